```python
import jax, jax.numpy as jnp
from jax import lax
import numpy as np

D_MODEL = 1024
BATCH = 16
SEQ = 2048
DEPTH = 2
DEC_BATCH = 32
DEC_SEQ = 4
PAST_LEN = 16384
PAGE_SIZE = 128

N_MIXERS = 2
HEAD_DIM = 64
N_HEADS = D_MODEL // HEAD_DIM
N_RWKV_LAYERS = (DEPTH + N_MIXERS - 1) // N_MIXERS
N_SB_LAYERS = DEPTH // N_MIXERS
N_TOKEN_SHIFT_MIXES = 6
D_DECAY_LORA = 64
D_AAA_LORA = 64
D_GATE_LORA = 128
D_FF = ((8 * D_MODEL // 3 + 127) // 128) * 128
CONV_W = 3
Q_BLOCK = 128
RMS_EPS = 1e-6
GN_EPS = 64e-5
POOL_SPARE_NUM = 5
POOL_SPARE_DEN = 4
SB_QK_BIAS_NORM = 9.8

kernel_name = 'rwkv7_stickbreaking_convffn_step'


def rms_norm(x, g):
    xf = x.astype(jnp.float32)
    y = xf * lax.rsqrt(jnp.mean(xf * xf, axis=-1, keepdims=True) + RMS_EPS)
    return (y * g.astype(jnp.float32)).astype(x.dtype)


def wkv7_scan(s0, r, decay, k, v, a_vec, b_vec):
    def step(S, inp):
        r_t, w_t, k_t, v_t, a_t, b_t = inp
        sa = jnp.einsum('bhvk,bhk->bhv', S, a_t)
        S = S * w_t[:, :, None, :] + sa[..., None] * b_t[:, :, None, :] + v_t[..., None] * k_t[:, :, None, :]
        return S, jnp.einsum('bhvk,bhk->bhv', S, r_t)
    xs = tuple(jnp.swapaxes(t, 0, 1) for t in (r, decay, k, v, a_vec, b_vec))
    s_final, ys = lax.scan(step, s0.astype(jnp.float32), xs)
    return s_final, jnp.swapaxes(ys, 0, 1)


def rwkv7_time_mix(x, shift0, s0, P, j):
    B, T, D = x.shape
    f32 = jnp.float32
    x_prev = jnp.concatenate([shift0[:, None, :].astype(x.dtype), x[:, :-1]], axis=1)
    xx = x_prev - x
    mu = P['tm_mu'][j]
    xr, xw, xk, xv, xa, xg = [x + xx * mu[i] for i in range(N_TOKEN_SHIFT_MIXES)]
    r = xr @ P['tm_w_r'][j]
    k = xk @ P['tm_w_k'][j]
    v = xv @ P['tm_w_v'][j]
    w_log = -jax.nn.softplus(-(P['tm_w0'][j] + jnp.tanh(xw @ P['tm_w_w1'][j]) @ P['tm_w_w2'][j])) - 0.5
    decay = jnp.exp(-jnp.exp(w_log.astype(f32)))
    a = jax.nn.sigmoid(P['tm_a0'][j] + (xa @ P['tm_w_a1'][j]) @ P['tm_w_a2'][j])
    g = jax.nn.sigmoid(xg @ P['tm_w_g1'][j]) @ P['tm_w_g2'][j]

    def heads(t):
        return t.astype(f32).reshape(B, T, N_HEADS, HEAD_DIM)

    kk = heads(k * P['tm_k_k'][j])
    kk = kk / jnp.maximum(jnp.sqrt(jnp.sum(kk * kk, axis=-1, keepdims=True)), 1e-12)
    k_h = heads(k * (1.0 + (a - 1.0) * P['tm_k_a'][j]))
    r_h, v_h, a_h, w_h = heads(r), heads(v), heads(a), heads(decay)
    s_final, y = wkv7_scan(s0, r_h, w_h, k_h, v_h, -kk, kk * a_h)
    mean = jnp.mean(y, axis=-1, keepdims=True)
    var = jnp.mean(jnp.square(y - mean), axis=-1, keepdims=True)
    y = (y - mean) * lax.rsqrt(var + GN_EPS)
    y = y * P['tm_gn_w'][j].astype(f32).reshape(N_HEADS, HEAD_DIM) + P['tm_gn_b'][j].astype(f32).reshape(N_HEADS, HEAD_DIM)
    y = y + jnp.sum(r_h * k_h * P['tm_r_k'][j].astype(f32), axis=-1, keepdims=True) * v_h
    out = (y.reshape(B, T, D).astype(x.dtype) * g) @ P['tm_w_o'][j]
    return out.astype(x.dtype), x[:, -1], s_final


def stick_breaking_attention(q, k, v, q_offset):
    Tq = q.shape[1]
    scale = HEAD_DIM ** -0.5
    outs = []
    for start in range(0, Tq, Q_BLOCK):
        end = min(start + Q_BLOCK, Tq)
        n_keys = q_offset + end
        k_blk, v_blk = k[:, :n_keys], v[:, :n_keys]
        z = jnp.einsum('bqhd,bkhd->bhqk', q[:, start:end], k_blk).astype(jnp.float32) * scale
        q_pos = q_offset + jnp.arange(start, end)
        k_pos = jnp.arange(n_keys)
        causal = k_pos[None, :] < q_pos[:, None]
        log_beta = jax.nn.log_sigmoid(z)
        log_keep = jnp.where(causal, log_beta - z, 0.0)
        log_after = lax.cumsum(log_keep, axis=3, reverse=True) - log_keep
        att = jnp.where(causal, jnp.exp(log_beta + log_after), 0.0)
        outs.append(jnp.einsum('bhqk,bkhd->bqhd', att.astype(v.dtype), v_blk))
    return jnp.concatenate(outs, axis=1)


def gather_pages(pool, page_table):
    rows = pool[page_table]
    b, n_pages = page_table.shape
    return rows.reshape(b, n_pages * pool.shape[1], N_HEADS, HEAD_DIM)


def stick_breaking_layer(x, k_past, v_past, w_qkv, b_qkv, w_o):
    B, T, D = x.shape
    qkv = (x @ w_qkv + b_qkv).reshape(B, T, 3, N_HEADS, HEAD_DIM)
    q, k, v = qkv[:, :, 0], qkv[:, :, 1], qkv[:, :, 2]
    if k_past is None:
        keys, vals, offset = k, v, 0
    else:
        keys = jnp.concatenate([k_past.astype(k.dtype), k], axis=1)
        vals = jnp.concatenate([v_past.astype(v.dtype), v], axis=1)
        offset = k_past.shape[1]
    o = stick_breaking_attention(q, keys, vals, offset)
    return (o.reshape(B, T, D) @ w_o).astype(x.dtype), k, v


def conv_ffn(x, conv_buf, w_gate, w_up, conv_w, conv_b, w_down):
    T = x.shape[1]
    a = x @ w_gate
    u = x @ w_up
    ap = jnp.concatenate([conv_buf.astype(a.dtype), a], axis=1)
    c = conv_b + sum(ap[:, i:i + T] * conv_w[i] for i in range(CONV_W))
    h = jax.nn.silu(c) * u
    return (h @ w_down).astype(x.dtype), ap[:, -(CONV_W - 1):]


def run_trunk(x, shift_in, wkv_in, conv_in, cache_k, cache_v, page_table, P):
    h = x
    new_shift, new_wkv, new_k, new_v, new_conv = [], [], [], [], []
    for i in range(DEPTH):
        j = i // N_MIXERS
        hn = rms_norm(h, P['norm_mix'][i])
        if i % N_MIXERS == 0:
            y, s_last, S = rwkv7_time_mix(hn, shift_in[j], wkv_in[j], P, j)
            new_shift.append(s_last)
            new_wkv.append(S)
        else:
            if page_table is None:
                kp, vp = None, None
            else:
                kp = gather_pages(cache_k[j], page_table)
                vp = gather_pages(cache_v[j], page_table)
            y, k_new, v_new = stick_breaking_layer(hn, kp, vp, P['sb_w_qkv'][j], P['sb_b_qkv'][j], P['sb_w_o'][j])
            new_k.append(k_new)
            new_v.append(v_new)
        h = h + y
        hn = rms_norm(h, P['norm_ffn'][i])
        y, buf = conv_ffn(hn, conv_in[i], P['ffn_w_gate'][i], P['ffn_w_up'][i], P['ffn_conv_w'][i],
                          P['ffn_conv_b'][i], P['ffn_w_down'][i])
        new_conv.append(buf)
        h = h + y
    return (rms_norm(h, P['norm_final']), jnp.stack(new_shift), jnp.stack(new_wkv),
            jnp.stack(new_k), jnp.stack(new_v), jnp.stack(new_conv))


def setup_inputs(seed: int = 0) -> dict:
    key = jax.random.key(seed)
    keys = jax.random.split(key, 48)
    ctr = [0]

    def nk():
        ctr[0] += 1
        return keys[ctr[0] - 1]

    def nrm(shape, scale):
        return jax.random.normal(nk(), shape, jnp.float32) * scale

    D, H, F = D_MODEL, N_HEADS, D_FF
    n_pages = PAST_LEN // PAGE_SIZE
    n_used = DEC_BATCH * n_pages
    n_phys = (n_used * POOL_SPARE_NUM + POOL_SPARE_DEN - 1) // POOL_SPARE_DEN
    u = nrm((N_SB_LAYERS, H, HEAD_DIM), 1.0)
    u = u / jnp.sqrt(jnp.sum(u * u, axis=-1, keepdims=True))
    b_q = -SB_QK_BIAS_NORM * u
    b_k = SB_QK_BIAS_NORM * u
    b_v = nrm((N_SB_LAYERS, H, HEAD_DIM), 0.02)
    sb_b_qkv = jnp.stack([b_q, b_k, b_v], axis=1).reshape(N_SB_LAYERS, 3 * D)
    x_prompt = nrm((BATCH, SEQ, D), 1.0)
    x_sample = nrm((DEC_BATCH, DEC_SEQ, D), 1.0)
    state_shift = nrm((N_RWKV_LAYERS, DEC_BATCH, D), 1.0)
    state_wkv = nrm((N_RWKV_LAYERS, DEC_BATCH, H, HEAD_DIM, HEAD_DIM), 0.3)
    cache_k = nrm((N_SB_LAYERS, n_phys, PAGE_SIZE, H, HEAD_DIM), 1.0) + b_k[:, None, None]
    cache_v = nrm((N_SB_LAYERS, n_phys, PAGE_SIZE, H, HEAD_DIM), 1.0)
    state_conv = nrm((DEPTH, DEC_BATCH, CONV_W - 1, F), 1.0)
    page_table = jax.random.permutation(nk(), n_phys)[:n_used].reshape(DEC_BATCH, n_pages).astype(jnp.int32)
    L = N_RWKV_LAYERS
    return {
        'x_prompt': x_prompt, 'x_sample': x_sample,
        'state_shift': state_shift, 'state_wkv': state_wkv,
        'cache_k': cache_k, 'cache_v': cache_v, 'state_conv': state_conv, 'page_table': page_table,
        'norm_mix': 1.0 + nrm((DEPTH, D), 0.02),
        'norm_ffn': 1.0 + nrm((DEPTH, D), 0.02),
        'norm_final': 1.0 + nrm((D,), 0.02),
        'tm_mu': jax.random.uniform(nk(), (L, N_TOKEN_SHIFT_MIXES, D), jnp.float32),
        'tm_w_r': nrm((L, D, D), D ** -0.5),
        'tm_w_k': nrm((L, D, D), D ** -0.5),
        'tm_w_v': nrm((L, D, D), D ** -0.5),
        'tm_w_o': nrm((L, D, D), D ** -0.5),
        'tm_w0': nrm((L, D), 0.5),
        'tm_w_w1': nrm((L, D, D_DECAY_LORA), D ** -0.5),
        'tm_w_w2': nrm((L, D_DECAY_LORA, D), D_DECAY_LORA ** -0.5),
        'tm_a0': nrm((L, D), 0.5),
        'tm_w_a1': nrm((L, D, D_AAA_LORA), D ** -0.5),
        'tm_w_a2': nrm((L, D_AAA_LORA, D), D_AAA_LORA ** -0.5),
        'tm_w_g1': nrm((L, D, D_GATE_LORA), D ** -0.5),
        'tm_w_g2': nrm((L, D_GATE_LORA, D), D_GATE_LORA ** -0.5),
        'tm_k_k': 0.85 + nrm((L, D), 0.05),
        'tm_k_a': 1.0 + nrm((L, D), 0.05),
        'tm_r_k': nrm((L, H, HEAD_DIM), 0.1),
        'tm_gn_w': 1.0 + nrm((L, D), 0.02),
        'tm_gn_b': nrm((L, D), 0.02),
        'sb_w_qkv': nrm((N_SB_LAYERS, D, 3 * D), D ** -0.5),
        'sb_b_qkv': sb_b_qkv,
        'sb_w_o': nrm((N_SB_LAYERS, D, D), D ** -0.5),
        'ffn_w_gate': nrm((DEPTH, D, F), D ** -0.5),
        'ffn_w_up': nrm((DEPTH, D, F), D ** -0.5),
        'ffn_conv_w': nrm((DEPTH, CONV_W, F), CONV_W ** -0.5),
        'ffn_conv_b': nrm((DEPTH, F), 0.02),
        'ffn_w_down': nrm((DEPTH, F, D), F ** -0.5),
    }


def reference(x_prompt, x_sample, state_shift, state_wkv, cache_k, cache_v, state_conv, page_table,
              norm_mix, norm_ffn, norm_final, tm_mu, tm_w_r, tm_w_k, tm_w_v, tm_w_o, tm_w0, tm_w_w1, tm_w_w2,
              tm_a0, tm_w_a1, tm_w_a2, tm_w_g1, tm_w_g2, tm_k_k, tm_k_a, tm_r_k, tm_gn_w, tm_gn_b,
              sb_w_qkv, sb_b_qkv, sb_w_o, ffn_w_gate, ffn_w_up, ffn_conv_w, ffn_conv_b, ffn_w_down):
    P = {
        'norm_mix': norm_mix, 'norm_ffn': norm_ffn, 'norm_final': norm_final,
        'tm_mu': tm_mu, 'tm_w_r': tm_w_r, 'tm_w_k': tm_w_k, 'tm_w_v': tm_w_v, 'tm_w_o': tm_w_o,
        'tm_w0': tm_w0, 'tm_w_w1': tm_w_w1, 'tm_w_w2': tm_w_w2,
        'tm_a0': tm_a0, 'tm_w_a1': tm_w_a1, 'tm_w_a2': tm_w_a2,
        'tm_w_g1': tm_w_g1, 'tm_w_g2': tm_w_g2, 'tm_k_k': tm_k_k, 'tm_k_a': tm_k_a, 'tm_r_k': tm_r_k,
        'tm_gn_w': tm_gn_w, 'tm_gn_b': tm_gn_b,
        'sb_w_qkv': sb_w_qkv, 'sb_b_qkv': sb_b_qkv, 'sb_w_o': sb_w_o,
        'ffn_w_gate': ffn_w_gate, 'ffn_w_up': ffn_w_up, 'ffn_conv_w': ffn_conv_w,
        'ffn_conv_b': ffn_conv_b, 'ffn_w_down': ffn_w_down,
    }
    b_p = x_prompt.shape[0]
    shift0 = jnp.zeros((N_RWKV_LAYERS, b_p, D_MODEL), x_prompt.dtype)
    wkv0 = jnp.zeros((N_RWKV_LAYERS, b_p, N_HEADS, HEAD_DIM, HEAD_DIM), jnp.float32)
    conv0 = jnp.zeros((DEPTH, b_p, CONV_W - 1, D_FF), x_prompt.dtype)
    y_prompt, shift_p, wkv_p, k_p, v_p, conv_p = run_trunk(x_prompt, shift0, wkv0, conv0, None, None, None, P)
    y_sample, shift_s, wkv_s, k_s, v_s, conv_s = run_trunk(x_sample, state_shift, state_wkv, state_conv,
                                                           cache_k, cache_v, page_table, P)
    return (y_prompt, y_sample, shift_p, wkv_p, k_p, v_p, conv_p, shift_s, wkv_s, k_s, v_s, conv_s)
```

```python
import functools
import math

import jax
import jax.numpy as jnp
from jax import lax
from jax.experimental import pallas as pl
from jax.experimental.pallas import tpu as pltpu

HEAD_DIM = 64
LANES = 128
SUBLANES = 8
HEADS_PER_TILE = LANES // HEAD_DIM
CONV_W = 3
RMS_EPS = 1e-6
GN_EPS = 64e-5
KK_EPS = 1e-12
N_MIX = 6
VMEM_LIMIT = 56 * 1024 * 1024

F32 = jnp.float32
BF16 = jnp.bfloat16


def _bf(x):
    return x.astype(BF16)


def _indicator(mask, dtype=F32):
    return jnp.where(mask, 1.0, 0.0).astype(dtype)


def _dot(a, b):
    return jnp.dot(a, b, preferred_element_type=F32)


def _dot_nt(a, b):
    return lax.dot_general(a, b, (((1,), (1,)), ((), ())), preferred_element_type=F32)


def _dot_tn(a, b):
    return lax.dot_general(a, b, (((0,), (0,)), ((), ())), preferred_element_type=F32)


def _split2(x):
    hi = x.astype(BF16)
    lo = (x - hi.astype(F32)).astype(BF16)
    return hi, lo


def _split3(x):
    hi = x.astype(BF16)
    r1 = x - hi.astype(F32)
    mid = r1.astype(BF16)
    lo = (r1 - mid.astype(F32)).astype(BF16)
    return hi, mid, lo


def _dot_sel(x, m):
    hi, lo = _split2(x)
    return _dot(hi, m) + _dot(lo, m)


def _dot_hp(a, b, dot=_dot):
    ah, al = _split2(a)
    bh, bl = _split2(b)
    return dot(ah, bh) + (dot(ah, bl) + dot(al, bh))


def _rms(x, g):
    return x * lax.rsqrt(jnp.mean(x * x, axis=-1, keepdims=True) + RMS_EPS) * g


def _softplus(x):
    return jnp.maximum(x, 0.0) + jnp.log1p(jnp.exp(-jnp.abs(x)))


def _sigmoid(x):
    return 1.0 / (1.0 + jnp.exp(-x))


def _log_sigmoid(x):
    return -_softplus(-x)


def _head_ones():
    r = lax.broadcasted_iota(jnp.int32, (LANES, LANES), 0) // HEAD_DIM
    c = lax.broadcasted_iota(jnp.int32, (LANES, LANES), 1) // HEAD_DIM
    return _indicator(r == c, BF16)


def _head_sum(x, ones_bd):
    d = x.shape[-1]
    cols = [_dot_sel(x[:, g * LANES:(g + 1) * LANES], ones_bd) for g in range(d // LANES)]
    return jnp.concatenate(cols, axis=1)


def _const_spec(shape):
    nd = len(shape)
    return pl.BlockSpec(shape, lambda *_: (0,) * nd, pipeline_mode=pl.Buffered(1))


def _params(sem):
    return pltpu.CompilerParams(dimension_semantics=sem, vmem_limit_bytes=VMEM_LIMIT)


def _tm_pre_kernel(x_ref, init_ref, gmix_ref, mu_ref, wr_ref, wk_ref, wv_ref, w1_ref, w2_ref, w0_ref,
                   a1_ref, a2_ref, a0_ref, g1_ref, g2_ref, kk_ref, ka_ref,
                   r_out, lw_out, kh_out, v_out, kk_out, kka_out, g_out, shift_out, hbuf, *, step, pad, tm):
    t = pl.program_id(1)
    hn = _rms(x_ref[0], gmix_ref[...])

    @pl.when(t == 0)
    def _():
        hbuf[pad - step:pad, :] = init_ref[0]

    @pl.when(t > 0)
    def _():
        hbuf[pad - step:pad, :] = hbuf[pad + tm - step:pad + tm, :]

    hbuf[pad:pad + tm, :] = hn
    xx = hbuf[pad - step:pad - step + tm, :] - hn
    mu = mu_ref[...]
    xr, xw, xk, xv, xa, xg = [_bf(hn + xx * mu[i:i + 1, :]) for i in range(N_MIX)]
    r = _dot(xr, wr_ref[...])
    k = _dot(xk, wk_ref[...])
    v = _dot(xv, wv_ref[...])
    wl = w0_ref[...] + _dot(_bf(jnp.tanh(_dot(xw, w1_ref[...]))), w2_ref[...])
    w_log = -_softplus(-wl) - 0.5
    a = _sigmoid(a0_ref[...] + _dot(_bf(_dot(xa, a1_ref[...])), a2_ref[...]))
    g = _dot(_bf(_sigmoid(_dot(xg, g1_ref[...]))), g2_ref[...])
    kk = k * kk_ref[...]
    nrm = jnp.sqrt(_head_sum(kk * kk, _head_ones()))
    kk = kk / jnp.maximum(nrm, KK_EPS)
    r_out[0] = r
    lw_out[0] = -jnp.exp(w_log)
    kh_out[0] = k * (1.0 + (a - 1.0) * ka_ref[...])
    v_out[0] = v
    kk_out[0] = kk
    kka_out[0] = kk * a
    g_out[0] = g
    shift_out[0] = hn[tm - step:, :]


def _tm_pre(x, init, w, *, step, tm):
    nb, tt, d = x.shape
    pad = max(SUBLANES, step)
    nt = tt // tm
    row = pl.BlockSpec((1, tm, d), lambda b, t: (b, t, 0))
    weights = [w['gmix'], w['mu'], w['wr'], w['wk'], w['wv'], w['w1'], w['w2'], w['w0'],
               w['a1'], w['a2'], w['a0'], w['g1'], w['g2'], w['k_k'], w['k_a']]
    outs = pl.pallas_call(
        functools.partial(_tm_pre_kernel, step=step, pad=pad, tm=tm),
        grid=(nb, nt),
        in_specs=[row, pl.BlockSpec((1, step, d), lambda b, t: (b, 0, 0))] + [_const_spec(a.shape) for a in weights],
        out_specs=[row] * 7 + [pl.BlockSpec((1, step, d), lambda b, t: (b, 0, 0))],
        out_shape=[jax.ShapeDtypeStruct((nb, tt, d), F32)] * 7 + [jax.ShapeDtypeStruct((nb, step, d), F32)],
        scratch_shapes=[pltpu.VMEM((pad + tm, d), F32)],
        compiler_params=_params(("arbitrary", "arbitrary")),
        name="tm_pre",
    )(x, init, *weights)
    return outs


def _wkv_kernel(r_ref, lw_ref, k_ref, v_ref, kk_ref, kka_ref, s0_ref, y_ref, sT_ref, s_scr, *, c):
    ci = pl.program_id(2)
    c2 = 2 * c
    lane_head = lax.broadcasted_iota(jnp.int32, (1, LANES), 1) // HEAD_DIM
    m0 = _indicator(lane_head == 0)
    m1 = 1.0 - m0
    hr = lax.broadcasted_iota(jnp.int32, (LANES, LANES), 0) // HEAD_DIM
    hc = lax.broadcasted_iota(jnp.int32, (LANES, LANES), 1) // HEAD_DIM
    same_head = hr == hc

    @pl.when(ci == 0)
    def _():
        s_scr[...] = s0_ref[0, 0]

    r = r_ref[0]
    lw = lw_ref[0]
    k = k_ref[0]
    v = v_ref[0]
    a = -kk_ref[0]
    b = kka_ref[0]
    s0 = s_scr[...]

    trow = lax.broadcasted_iota(jnp.int32, (c, c), 0)
    tcol = lax.broadcasted_iota(jnp.int32, (c, c), 1)
    lower = _indicator(tcol <= trow, BF16)
    h1, h2, h3 = _split3(lw)
    cum = _dot(lower, h1) + (_dot(lower, h2) + _dot(lower, h3))
    cum_c = cum[c - 1:c, :]
    a_t = a * jnp.exp(cum - lw)
    r_t = r * jnp.exp(cum)
    einv = jnp.exp(-cum)
    b_t = b * einv
    k_t = k * einv
    eend = jnp.exp(cum_c - cum)
    b_e = b * eend
    k_e = k * eend

    rr = lax.broadcasted_iota(jnp.int32, (c2, c2), 0)
    cc = lax.broadcasted_iota(jnp.int32, (c2, c2), 1)
    ti = rr % c
    tj = cc % c
    top = rr < c
    left = cc < c
    keep0 = tj < ti + jnp.where(top, 0, 1)
    keep1 = tj < ti + jnp.where(top, 1, 0)
    sc0 = _dot_hp(jnp.concatenate([a_t * m0, r_t * m0], axis=0), jnp.concatenate([b_t, k_t], axis=0), _dot_nt)
    sc1 = _dot_hp(jnp.concatenate([r_t * m1, a_t * m1], axis=0), jnp.concatenate([k_t, b_t], axis=0), _dot_nt)
    sc0 = jnp.where(keep0, sc0, 0.0)
    sc1 = jnp.where(keep1, sc1, 0.0)

    ar_s = _dot_hp(jnp.concatenate([a_t, r_t], axis=0), s0, _dot_nt)
    zeros = jnp.zeros((c, LANES), F32)
    v0 = v * m0
    v1 = v * m1
    rhs = (ar_s[:c] + _dot_hp(sc0[:c], jnp.concatenate([zeros, v0], axis=0))
           + _dot_hp(sc1[c:], jnp.concatenate([v1, zeros], axis=0)))

    n_bd = jnp.where(top & left, sc0, 0.0) + jnp.where((~top) & (~left), sc1, 0.0)
    eye = _indicator(rr == cc)
    t_inv = eye + n_bd
    p = n_bd
    for _ in range(int(math.log2(c)) - 1):
        p = _dot_hp(p, p)
        t_inv = t_inv + _dot_hp(t_inv, p)
    ust = _dot_hp(t_inv, jnp.concatenate([rhs * m0, rhs * m1], axis=0))
    u0 = ust[:c]
    u1 = ust[c:]

    y = (ar_s[c:] + _dot_hp(sc0[c:], jnp.concatenate([u0, v0], axis=0))
         + _dot_hp(sc1[:c], jnp.concatenate([v1, u1], axis=0)))
    y_ref[0] = y

    upd = _dot_hp(jnp.concatenate([u0 + u1, v], axis=0), jnp.concatenate([b_e, k_e], axis=0), _dot_tn)
    s_new = s0 * jnp.exp(cum_c) + jnp.where(same_head, upd, 0.0)
    s_scr[...] = s_new

    @pl.when(ci == pl.num_programs(2) - 1)
    def _():
        sT_ref[0, 0] = s_new


def _wkv(r, lw, kh, v, kk, kka, s0, *, c):
    bsz, t, d = r.shape
    n_pairs = d // LANES
    hp = HEADS_PER_TILE
    eye = jnp.eye(hp, dtype=F32)
    s0_bd = jnp.einsum('bphvk,hg->bphvgk', s0.reshape(bsz, n_pairs, hp, HEAD_DIM, HEAD_DIM), eye)
    s0_bd = s0_bd.reshape(bsz, n_pairs, LANES, LANES)
    seq = pl.BlockSpec((1, c, LANES), lambda b, p, i: (b, i, p))
    st = pl.BlockSpec((1, 1, LANES, LANES), lambda b, p, i: (b, p, 0, 0))
    y, s_bd = pl.pallas_call(
        functools.partial(_wkv_kernel, c=c),
        grid=(bsz, n_pairs, t // c),
        in_specs=[seq] * 6 + [st],
        out_specs=[seq, st],
        out_shape=[jax.ShapeDtypeStruct((bsz, t, d), F32), jax.ShapeDtypeStruct(s0_bd.shape, F32)],
        scratch_shapes=[pltpu.VMEM((LANES, LANES), F32)],
        compiler_params=_params(("arbitrary", "arbitrary", "arbitrary")),
        name="wkv",
    )(r, lw, kh, v, kk, kka, s0_bd)
    s_bd = s_bd.reshape(bsz, n_pairs, hp, HEAD_DIM, hp, HEAD_DIM)
    s_fin = jnp.stack([s_bd[:, :, h, :, h, :] for h in range(hp)], axis=2)
    return y, s_fin.reshape(s0.shape)


def _tm_post_kernel(y_ref, r_ref, kh_ref, v_ref, g_ref, x_ref, gnw_ref, gnb_ref, rk_ref, wo_ref, out_ref):
    ones_bd = _head_ones()
    y = y_ref[...]
    mean = _head_sum(y, ones_bd) * (1.0 / HEAD_DIM)
    yc = y - mean
    var = _head_sum(yc * yc, ones_bd) * (1.0 / HEAD_DIM)
    yn = yc * lax.rsqrt(var + GN_EPS) * gnw_ref[...] + gnb_ref[...]
    bonus = _head_sum(r_ref[...] * kh_ref[...] * rk_ref[...], ones_bd) * v_ref[...]
    out_ref[...] = x_ref[...] + _dot(_bf((yn + bonus) * g_ref[...]), wo_ref[...])


def _tm_post(y, r, kh, v, g, x, w, *, tm):
    n, d = x.shape
    row = pl.BlockSpec((tm, d), lambda i: (i, 0))
    weights = [w['gn_w'], w['gn_b'], w['r_k'], w['wo']]
    return pl.pallas_call(
        _tm_post_kernel,
        grid=(n // tm,),
        in_specs=[row] * 6 + [_const_spec(a.shape) for a in weights],
        out_specs=row,
        out_shape=jax.ShapeDtypeStruct((n, d), F32),
        compiler_params=_params(("arbitrary",)),
        name="tm_post",
    )(y, r, kh, v, g, x, *weights)


def _ffn_kernel(h_ref, init_ref, gn_ref, wg_ref, wu_ref, cw_ref, cb_ref, wd_ref, gfin_ref,
                out_ref, conv_out, abuf, *, step, pad, tm, final_norm):
    t = pl.program_id(1)
    hist = (CONV_W - 1) * step
    h = h_ref[0]
    hb = _bf(_rms(h, gn_ref[...]))

    @pl.when(t == 0)
    def _():
        abuf[pad - hist:pad, :] = init_ref[0]

    @pl.when(t > 0)
    def _():
        abuf[pad - hist:pad, :] = abuf[pad + tm - hist:pad + tm, :]

    a = _dot(hb, wg_ref[...])
    abuf[pad:pad + tm, :] = a
    cw = cw_ref[...]
    conv = cb_ref[...] + a * cw[CONV_W - 1:CONV_W, :]
    for i in range(CONV_W - 1):
        off = pad - (CONV_W - 1 - i) * step
        conv = conv + abuf[off:off + tm, :] * cw[i:i + 1, :]
    u = _dot(hb, wu_ref[...])
    act = conv * _sigmoid(conv) * u
    out = h + _dot(_bf(act), wd_ref[...])
    if final_norm:
        out = _rms(out, gfin_ref[...])
    out_ref[0] = out
    conv_out[0] = abuf[pad + tm - hist:pad + tm, :]


def _ffn(h, init, w, gfin, *, step, tm, final_norm):
    nb, tt, d = h.shape
    f = w['wg'].shape[1]
    hist = (CONV_W - 1) * step
    pad = max(SUBLANES, hist)
    row = pl.BlockSpec((1, tm, d), lambda b, t: (b, t, 0))
    cst = pl.BlockSpec((1, hist, f), lambda b, t: (b, 0, 0))
    weights = [w['gn'], w['wg'], w['wu'], w['cw'], w['cb'], w['wd'], gfin]
    return pl.pallas_call(
        functools.partial(_ffn_kernel, step=step, pad=pad, tm=tm, final_norm=final_norm),
        grid=(nb, tt // tm),
        in_specs=[row, cst] + [_const_spec(a.shape) for a in weights],
        out_specs=[row, cst],
        out_shape=[jax.ShapeDtypeStruct((nb, tt, d), F32), jax.ShapeDtypeStruct((nb, hist, f), F32)],
        scratch_shapes=[pltpu.VMEM((pad + tm, f), F32)],
        compiler_params=_params(("arbitrary", "arbitrary")),
        name="ffn_final" if final_norm else "ffn",
    )(h, init, *weights)


def _qkv_kernel(h_ref, gn_ref, w_ref, b_ref, q_out, k_out, v_out):
    d = h_ref.shape[-1]
    hb = _bf(_rms(h_ref[...], gn_ref[...]))
    qkv = _dot(hb, w_ref[...]) + b_ref[...]
    q_out[...] = qkv[:, :d]
    k_out[...] = qkv[:, d:2 * d]
    v_out[...] = qkv[:, 2 * d:]


def _qkv(h, w, *, tm):
    n, d = h.shape
    row = pl.BlockSpec((tm, d), lambda i: (i, 0))
    weights = [w['gn'], w['wqkv'], w['bqkv']]
    return pl.pallas_call(
        _qkv_kernel,
        grid=(n // tm,),
        in_specs=[row] + [_const_spec(a.shape) for a in weights],
        out_specs=[row] * 3,
        out_shape=[jax.ShapeDtypeStruct((n, d), F32)] * 3,
        compiler_params=_params(("arbitrary",)),
        name="sb_qkv",
    )(h, *weights)


def _suffix_ones(n):
    r = lax.broadcasted_iota(jnp.int32, (n, 2 * n), 0)
    c = lax.broadcasted_iota(jnp.int32, (n, 2 * n), 1)
    return _indicator((r >= c) | (c >= n), BF16)


def _stick_block(qm, kb, vb, valid, carry, tri, keys_minor=False):
    nk = tri.shape[0]
    z = _dot(qm, kb) if keys_minor else _dot_nt(qm, kb)
    lb = _log_sigmoid(z)
    lk = lb - z
    if valid is not None:
        lk = jnp.where(valid, lk, 0.0)
    cs = _dot_sel(lk, tri)
    la = cs[:, :nk] - lk + carry
    att = jnp.exp(lb + la)
    if valid is not None:
        att = jnp.where(valid, att, 0.0)
    pv = _dot_nt(_bf(att), vb) if keys_minor else _dot(_bf(att), vb)
    return pv, carry + cs[:, nk:]


def _attn_kernel(q_ref, k_ref, v_ref, o_ref, acc_ref, carry_ref, *, qb):
    i = pl.program_id(2)
    lane_head = lax.broadcasted_iota(jnp.int32, (1, LANES), 1) // HEAD_DIM
    q = q_ref[0] * (HEAD_DIM ** -0.5)
    tri = _suffix_ones(qb)
    row = lax.broadcasted_iota(jnp.int32, (qb, qb), 0)
    col = lax.broadcasted_iota(jnp.int32, (qb, qb), 1)
    diag_valid = col < row
    acc_ref[...] = jnp.zeros(acc_ref.shape, F32)
    carry_ref[...] = jnp.zeros(carry_ref.shape, F32)
    qms = [_bf(jnp.where(lane_head == h, q, 0.0)) for h in range(HEADS_PER_TILE)]

    def block(j, valid):
        start = pl.multiple_of(j * qb, qb)
        kb = _bf(k_ref[0, pl.ds(start, qb), :])
        vb = _bf(v_ref[0, pl.ds(start, qb), :])
        for h in range(HEADS_PER_TILE):
            pv, carry = _stick_block(qms[h], kb, vb, valid, carry_ref[h], tri)
            carry_ref[h] = carry
            acc_ref[...] += jnp.where(lane_head == h, pv, 0.0)

    block(i, diag_valid)

    def body(jj, _):
        block(i - 1 - jj, None)
        return 0

    lax.fori_loop(0, i, body, 0)
    o_ref[0] = acc_ref[...]


def _attn_prompt(q, k, v, *, qb):
    bsz, t, d = q.shape
    n_pairs = d // LANES
    qs = pl.BlockSpec((1, qb, LANES), lambda b, p, i: (b, i, p))
    kv = pl.BlockSpec((1, t, LANES), lambda b, p, i: (b, 0, p))
    return pl.pallas_call(
        functools.partial(_attn_kernel, qb=qb),
        grid=(bsz, n_pairs, t // qb),
        in_specs=[qs, kv, kv],
        out_specs=qs,
        out_shape=jax.ShapeDtypeStruct((bsz, t, d), F32),
        scratch_shapes=[pltpu.VMEM((qb, LANES), F32), pltpu.VMEM((HEADS_PER_TILE, qb, qb), F32)],
        compiler_params=_params(("arbitrary", "arbitrary", "arbitrary")),
        name="sb_attn",
    )(q, k, v)


def _paged_kernel(pt_ref, q_ref, kn_ref, vn_ref, kc_ref, vc_ref, o_ref, qexp, acc_ref, carry_ref, kpad, vpad,
                  *, tq, n_heads, page):
    p = pl.program_id(1)
    d = q_ref.shape[-1]
    nq = tq * n_heads
    tri = _suffix_ones(page)
    head_of_lane = lax.broadcasted_iota(jnp.int32, (n_heads, d), 1) // HEAD_DIM
    head_mask = head_of_lane == lax.broadcasted_iota(jnp.int32, (n_heads, d), 0)

    @pl.when(p == 0)
    def _():
        q = q_ref[0] * (HEAD_DIM ** -0.5)
        for t in range(tq):
            qexp[t * n_heads:(t + 1) * n_heads, :] = _bf(jnp.where(head_mask, q[t:t + 1, :], 0.0))
        kpad[...] = jnp.zeros(kpad.shape, F32)
        vpad[...] = jnp.zeros(vpad.shape, F32)
        kpad[0:tq, :] = kn_ref[0]
        vpad[0:tq, :] = vn_ref[0]
        row_t = lax.broadcasted_iota(jnp.int32, (nq, page), 0) // n_heads
        col = lax.broadcasted_iota(jnp.int32, (nq, page), 1)
        pv, carry = _stick_block(qexp[...], _bf(kpad[...]), _bf(vpad[...]), col < row_t,
                                 jnp.zeros((nq, page), F32), tri)
        acc_ref[...] = pv
        carry_ref[...] = carry

    pv, carry = _stick_block(qexp[...], _bf(kc_ref[0]), _bf(vc_ref[0]), None, carry_ref[...], tri,
                             keys_minor=True)
    acc = acc_ref[...] + pv
    acc_ref[...] = acc
    carry_ref[...] = carry

    @pl.when(p == pl.num_programs(1) - 1)
    def _():
        rows = [jnp.sum(jnp.where(head_mask, acc[t * n_heads:(t + 1) * n_heads, :], 0.0), axis=0, keepdims=True)
                for t in range(tq)]
        o_ref[0] = jnp.concatenate(rows, axis=0)


def _attn_paged(q, k_new, v_new, cache_k, cache_v, page_table):
    bsz, tq, d = q.shape
    n_heads = d // HEAD_DIM
    n_pages = page_table.shape[1]
    page = cache_k.shape[2]
    nq = tq * n_heads
    new = pl.BlockSpec((1, tq, d), lambda b, p, pt: (b, 0, 0))
    pg = pl.BlockSpec((1, d, page), lambda b, p, pt: (pt[b, n_pages - 1 - p], 0, 0))
    grid_spec = pltpu.PrefetchScalarGridSpec(
        num_scalar_prefetch=1,
        grid=(bsz, n_pages),
        in_specs=[new, new, new, pg, pg],
        out_specs=new,
        scratch_shapes=[pltpu.VMEM((nq, d), BF16), pltpu.VMEM((nq, d), F32), pltpu.VMEM((nq, page), F32),
                        pltpu.VMEM((page, d), F32), pltpu.VMEM((page, d), F32)],
    )
    return pl.pallas_call(
        functools.partial(_paged_kernel, tq=tq, n_heads=n_heads, page=page),
        grid_spec=grid_spec,
        out_shape=jax.ShapeDtypeStruct((bsz, tq, d), F32),
        compiler_params=_params(("arbitrary", "arbitrary")),
        name="sb_paged",
    )(page_table, q, k_new, v_new, cache_k, cache_v)


def _proj_res_kernel(x_ref, res_ref, w_ref, out_ref):
    out_ref[...] = res_ref[...] + _dot(_bf(x_ref[...]), w_ref[...])


def _proj_res(x, res, w, *, tm):
    n, d = x.shape
    row = pl.BlockSpec((tm, d), lambda i: (i, 0))
    return pl.pallas_call(
        _proj_res_kernel,
        grid=(n // tm,),
        in_specs=[row, row, _const_spec(w.shape)],
        out_specs=row,
        out_shape=jax.ShapeDtypeStruct((n, d), F32),
        compiler_params=_params(("arbitrary",)),
        name="sb_out",
    )(x, res, w)


def _weights(norm_mix, norm_ffn, norm_final, tm_mu, tm_w_r, tm_w_k, tm_w_v, tm_w_o, tm_w0, tm_w_w1, tm_w_w2,
             tm_a0, tm_w_a1, tm_w_a2, tm_w_g1, tm_w_g2, tm_k_k, tm_k_a, tm_r_k, tm_gn_w, tm_gn_b,
             sb_w_qkv, sb_b_qkv, sb_w_o, ffn_w_gate, ffn_w_up, ffn_conv_w, ffn_conv_b, ffn_w_down):
    row = lambda a: a.reshape(1, -1)
    tm = dict(gmix=row(norm_mix[0]), mu=tm_mu[0], wr=_bf(tm_w_r[0]), wk=_bf(tm_w_k[0]), wv=_bf(tm_w_v[0]),
              w1=_bf(tm_w_w1[0]), w2=_bf(tm_w_w2[0]), w0=row(tm_w0[0]), a1=_bf(tm_w_a1[0]), a2=_bf(tm_w_a2[0]),
              a0=row(tm_a0[0]), g1=_bf(tm_w_g1[0]), g2=_bf(tm_w_g2[0]), k_k=row(tm_k_k[0]), k_a=row(tm_k_a[0]),
              gn_w=row(tm_gn_w[0]), gn_b=row(tm_gn_b[0]), r_k=row(tm_r_k[0]), wo=_bf(tm_w_o[0]))
    sb = dict(gn=row(norm_mix[1]), wqkv=_bf(sb_w_qkv[0]), bqkv=row(sb_b_qkv[0]), wo=_bf(sb_w_o[0]))
    ffn = [dict(gn=row(norm_ffn[i]), wg=_bf(ffn_w_gate[i]), wu=_bf(ffn_w_up[i]), cw=ffn_conv_w[i],
                cb=row(ffn_conv_b[i]), wd=_bf(ffn_w_down[i])) for i in range(2)]
    return tm, sb, ffn, row(norm_final)


def _trunk_prompt(x, w_tm, w_sb, w_ffn, g_final, *, tm, chunk, qb):
    bsz, t, d = x.shape
    n = bsz * t
    f = w_ffn[0]['wg'].shape[1]
    flat = lambda a: a.reshape(n, d)
    r, lw, kh, v, kk, kka, g, shift = _tm_pre(x, jnp.zeros((bsz, 1, d), F32), w_tm, step=1, tm=tm)
    s0 = jnp.zeros((bsz, d // HEAD_DIM, HEAD_DIM, HEAD_DIM), F32)
    y, wkv = _wkv(r, lw, kh, v, kk, kka, s0, c=chunk)
    h = _tm_post(flat(y), flat(r), flat(kh), flat(v), flat(g), flat(x), w_tm, tm=tm)
    conv0 = jnp.zeros((bsz, CONV_W - 1, f), F32)
    h, conv_a = _ffn(h.reshape(bsz, t, d), conv0, w_ffn[0], g_final, step=1, tm=tm, final_norm=False)
    q, k, v2 = _qkv(flat(h), w_sb, tm=tm)
    o = _attn_prompt(q.reshape(bsz, t, d), k.reshape(bsz, t, d), v2.reshape(bsz, t, d), qb=qb)
    h = _proj_res(flat(o), flat(h), w_sb['wo'], tm=tm)
    out, conv_b = _ffn(h.reshape(bsz, t, d), conv0, w_ffn[1], g_final, step=1, tm=tm, final_norm=True)
    hd = (bsz, t, d // HEAD_DIM, HEAD_DIM)
    return (out, shift.reshape(1, bsz, d), wkv[None], k.reshape(hd)[None], v2.reshape(hd)[None],
            jnp.stack([conv_a, conv_b]))


def _trunk_sample(x, state_shift, state_wkv, state_conv, cache_k, cache_v, page_table, w_tm, w_sb, w_ffn, g_final,
                  *, chunk):
    bsz, t, d = x.shape
    n = bsz * t
    f = w_ffn[0]['wg'].shape[1]
    to_tm = lambda a: jnp.swapaxes(a, 0, 1).reshape(1, n, a.shape[-1])
    to_bm = lambda a: jnp.swapaxes(a.reshape(t, bsz, a.shape[-1]), 0, 1)
    xs = to_tm(x)
    r, lw, kh, v, kk, kka, g, shift = _tm_pre(xs, state_shift[0][None], w_tm, step=bsz, tm=n)
    pad_t = lambda a: jnp.pad(to_bm(a), ((0, 0), (0, chunk - t), (0, 0)))
    y, wkv = _wkv(pad_t(r), pad_t(lw), pad_t(kh), pad_t(v), pad_t(kk), pad_t(kka), state_wkv[0], c=chunk)
    y = to_tm(y[:, :t])
    flat = lambda a: a.reshape(n, d)
    h = _tm_post(flat(y), flat(r), flat(kh), flat(v), flat(g), flat(xs), w_tm, tm=n)
    conv_tm = lambda a: jnp.swapaxes(a, 0, 1).reshape(1, (CONV_W - 1) * bsz, f)
    conv_bm = lambda a: jnp.swapaxes(a.reshape(CONV_W - 1, bsz, f), 0, 1)
    h, conv_a = _ffn(h.reshape(1, n, d), conv_tm(state_conv[0]), w_ffn[0], g_final, step=bsz, tm=n,
                     final_norm=False)
    q, k, v2 = _qkv(flat(h), w_sb, tm=n)
    q, k, v2 = to_bm(q), to_bm(k), to_bm(v2)
    n_phys, page = cache_k.shape[1], cache_k.shape[2]
    keys_minor = lambda c: jnp.transpose(c[0], (0, 2, 3, 1)).reshape(n_phys, d, page)
    o = _attn_paged(q, k, v2, keys_minor(cache_k), keys_minor(cache_v), page_table)
    h = _proj_res(flat(to_tm(o)), flat(h), w_sb['wo'], tm=n)
    out, conv_b = _ffn(h.reshape(1, n, d), conv_tm(state_conv[1]), w_ffn[1], g_final, step=bsz, tm=n,
                       final_norm=True)
    hd = (bsz, t, d // HEAD_DIM, HEAD_DIM)
    return (to_bm(out[0]), shift.reshape(1, bsz, d), wkv[None], k.reshape(hd)[None], v2.reshape(hd)[None],
            jnp.stack([conv_bm(conv_a[0]), conv_bm(conv_b[0])]))


PROMPT_TILE = 256
WKV_CHUNK = 64
SAMPLE_WKV_CHUNK = 8
Q_BLOCK = 128


def kernel(x_prompt, x_sample, state_shift, state_wkv, cache_k, cache_v, state_conv, page_table, norm_mix, norm_ffn, norm_final, tm_mu, tm_w_r, tm_w_k, tm_w_v, tm_w_o, tm_w0, tm_w_w1, tm_w_w2, tm_a0, tm_w_a1, tm_w_a2, tm_w_g1, tm_w_g2, tm_k_k, tm_k_a, tm_r_k, tm_gn_w, tm_gn_b, sb_w_qkv, sb_b_qkv, sb_w_o, ffn_w_gate, ffn_w_up, ffn_conv_w, ffn_conv_b, ffn_w_down):
    w_tm, w_sb, w_ffn, g_final = _weights(
        norm_mix, norm_ffn, norm_final, tm_mu, tm_w_r, tm_w_k, tm_w_v, tm_w_o, tm_w0, tm_w_w1, tm_w_w2,
        tm_a0, tm_w_a1, tm_w_a2, tm_w_g1, tm_w_g2, tm_k_k, tm_k_a, tm_r_k, tm_gn_w, tm_gn_b,
        sb_w_qkv, sb_b_qkv, sb_w_o, ffn_w_gate, ffn_w_up, ffn_conv_w, ffn_conv_b, ffn_w_down)
    tile = min(PROMPT_TILE, x_prompt.shape[1])
    y_p, shift_p, wkv_p, k_p, v_p, conv_p = _trunk_prompt(
        x_prompt, w_tm, w_sb, w_ffn, g_final, tm=tile, chunk=min(WKV_CHUNK, x_prompt.shape[1]),
        qb=min(Q_BLOCK, x_prompt.shape[1]))
    y_s, shift_s, wkv_s, k_s, v_s, conv_s = _trunk_sample(
        x_sample, state_shift, state_wkv, state_conv, cache_k, cache_v, page_table, w_tm, w_sb, w_ffn, g_final,
        chunk=SAMPLE_WKV_CHUNK)
    return (y_p, y_s, shift_p, wkv_p, k_p, v_p, conv_p, shift_s, wkv_s, k_s, v_s, conv_s)
```

```python
import functools
import math

import jax
import jax.numpy as jnp
from jax import lax
from jax.experimental import pallas as pl
from jax.experimental.pallas import tpu as pltpu

HEAD_DIM = 64
LANES = 128
SUBLANES = 8
HEADS_PER_TILE = LANES // HEAD_DIM
CONV_W = 3
RMS_EPS = 1e-6
GN_EPS = 64e-5
KK_EPS = 1e-12
N_MIX = 6
VMEM_LIMIT = 56 * 1024 * 1024

F32 = jnp.float32
BF16 = jnp.bfloat16


def _bf(x):
    return x.astype(BF16)


def _indicator(mask, dtype=F32):
    return jnp.where(mask, 1.0, 0.0).astype(dtype)


def _dot(a, b):
    return jnp.dot(a, b, preferred_element_type=F32)


def _dot_nt(a, b):
    return lax.dot_general(a, b, (((1,), (1,)), ((), ())), preferred_element_type=F32)


def _dot_tn(a, b):
    return lax.dot_general(a, b, (((0,), (0,)), ((), ())), preferred_element_type=F32)


def _split2(x):
    hi = x.astype(BF16)
    lo = (x - hi.astype(F32)).astype(BF16)
    return hi, lo


def _split3(x):
    hi = x.astype(BF16)
    r1 = x - hi.astype(F32)
    mid = r1.astype(BF16)
    lo = (r1 - mid.astype(F32)).astype(BF16)
    return hi, mid, lo


def _dot_sel(x, m):
    hi, lo = _split2(x)
    return _dot(hi, m) + _dot(lo, m)


def _dot_hp(a, b, dot=_dot):
    ah, al = _split2(a)
    bh, bl = _split2(b)
    return dot(ah, bh) + (dot(ah, bl) + dot(al, bh))


def _rms(x, g):
    return x * lax.rsqrt(jnp.mean(x * x, axis=-1, keepdims=True) + RMS_EPS) * g


def _softplus(x):
    return jnp.maximum(x, 0.0) + jnp.log(1.0 + jnp.exp(-jnp.abs(x)))


def _sigmoid(x):
    return 1.0 / (1.0 + jnp.exp(-x))


def _log_sigmoid(x):
    return -_softplus(-x)


def _head_ones():
    r = lax.broadcasted_iota(jnp.int32, (LANES, LANES), 0) // HEAD_DIM
    c = lax.broadcasted_iota(jnp.int32, (LANES, LANES), 1) // HEAD_DIM
    return _indicator(r == c, BF16)


def _head_sum(x, ones_bd):
    d = x.shape[-1]
    cols = [_dot_sel(x[:, g * LANES:(g + 1) * LANES], ones_bd) for g in range(d // LANES)]
    return jnp.concatenate(cols, axis=1)


def _interleave(stage_generators):
    live = list(stage_generators)
    done = object()
    while live:
        live = [g for g in live if next(g, done) is not done]


def _const_spec(shape):
    nd = len(shape)
    return pl.BlockSpec(shape, lambda *_: (0,) * nd, pipeline_mode=pl.Buffered(1))


def _params(sem):
    return pltpu.CompilerParams(dimension_semantics=sem, vmem_limit_bytes=VMEM_LIMIT)


def _tm_pre_kernel(x_ref, init_ref, gmix_ref, mu_ref, wr_ref, wk_ref, wv_ref, w1_ref, w2_ref, w0_ref,
                   a1_ref, a2_ref, a0_ref, g1_ref, g2_ref, kk_ref, ka_ref,
                   r_out, lw_out, kh_out, v_out, kk_out, kka_out, g_out, shift_out, hbuf, *, step, pad, tm):
    t = pl.program_id(1)
    hn = _rms(x_ref[0], gmix_ref[...])

    @pl.when(t == 0)
    def _():
        hbuf[pad - step:pad, :] = init_ref[0]

    @pl.when(t > 0)
    def _():
        hbuf[pad - step:pad, :] = hbuf[pad + tm - step:pad + tm, :]

    hbuf[pad:pad + tm, :] = hn
    xx = hbuf[pad - step:pad - step + tm, :] - hn
    mu = mu_ref[...]
    xr, xw, xk, xv, xa, xg = [_bf(hn + xx * mu[i:i + 1, :]) for i in range(N_MIX)]
    r = _dot(xr, wr_ref[...])
    k = _dot(xk, wk_ref[...])
    v = _dot(xv, wv_ref[...])
    wl = w0_ref[...] + _dot(_bf(jnp.tanh(_dot(xw, w1_ref[...]))), w2_ref[...])
    w_log = -_softplus(-wl) - 0.5
    a = _sigmoid(a0_ref[...] + _dot(_bf(_dot(xa, a1_ref[...])), a2_ref[...]))
    g = _dot(_bf(_sigmoid(_dot(xg, g1_ref[...]))), g2_ref[...])
    kk = k * kk_ref[...]
    nrm = jnp.sqrt(_head_sum(kk * kk, _head_ones()))
    kk = kk / jnp.maximum(nrm, KK_EPS)
    r_out[0] = r
    lw_out[0] = -jnp.exp(w_log)
    kh_out[0] = k * (1.0 + (a - 1.0) * ka_ref[...])
    v_out[0] = v
    kk_out[0] = kk
    kka_out[0] = kk * a
    g_out[0] = g
    shift_out[0] = hn[tm - step:, :]


def _tm_pre(x, init, w, *, step, tm):
    nb, tt, d = x.shape
    pad = max(SUBLANES, step)
    nt = tt // tm
    row = pl.BlockSpec((1, tm, d), lambda b, t: (b, t, 0))
    weights = [w['gmix'], w['mu'], w['wr'], w['wk'], w['wv'], w['w1'], w['w2'], w['w0'],
               w['a1'], w['a2'], w['a0'], w['g1'], w['g2'], w['k_k'], w['k_a']]
    outs = pl.pallas_call(
        functools.partial(_tm_pre_kernel, step=step, pad=pad, tm=tm),
        grid=(nb, nt),
        in_specs=[row, pl.BlockSpec((1, step, d), lambda b, t: (b, 0, 0))] + [_const_spec(a.shape) for a in weights],
        out_specs=[row] * 7 + [pl.BlockSpec((1, step, d), lambda b, t: (b, 0, 0))],
        out_shape=[jax.ShapeDtypeStruct((nb, tt, d), F32)] * 7 + [jax.ShapeDtypeStruct((nb, step, d), F32)],
        scratch_shapes=[pltpu.VMEM((pad + tm, d), F32)],
        compiler_params=_params(("arbitrary", "arbitrary")),
        name="tm_pre",
    )(x, init, *weights)
    return outs


def _wkv_kernel(r_ref, lw_ref, k_ref, v_ref, kk_ref, kka_ref, s0_ref, y_ref, sT_ref, s_scr, *, c, tiles):
    ci = pl.program_id(2)
    c2 = 2 * c
    lane_head = lax.broadcasted_iota(jnp.int32, (1, LANES), 1) // HEAD_DIM
    m0 = _indicator(lane_head == 0)
    m1 = 1.0 - m0
    hr = lax.broadcasted_iota(jnp.int32, (LANES, LANES), 0) // HEAD_DIM
    hc = lax.broadcasted_iota(jnp.int32, (LANES, LANES), 1) // HEAD_DIM
    same_head = hr == hc
    trow = lax.broadcasted_iota(jnp.int32, (c, c), 0)
    tcol = lax.broadcasted_iota(jnp.int32, (c, c), 1)
    lower = _indicator(tcol <= trow, BF16)
    rr = lax.broadcasted_iota(jnp.int32, (c2, c2), 0)
    cc = lax.broadcasted_iota(jnp.int32, (c2, c2), 1)
    ti = rr % c
    tj = cc % c
    top = rr < c
    left = cc < c
    keep0 = tj < ti + jnp.where(top, 0, 1)
    keep1 = tj < ti + jnp.where(top, 1, 0)
    left_half = lax.broadcasted_iota(jnp.int32, (c, c2), 1) < c
    wide = c2 % LANES == 0
    n_double = int(math.log2(c))

    @pl.when(ci == 0)
    def _():
        s_scr[...] = s0_ref[0]

    def tile_stages(tile):
        lanes = slice(tile * LANES, (tile + 1) * LANES)
        r = r_ref[0, :, lanes]
        lw = lw_ref[0, :, lanes]
        k = k_ref[0, :, lanes]
        v = v_ref[0, :, lanes]
        a = -kk_ref[0, :, lanes]
        b = kka_ref[0, :, lanes]
        s0 = s_scr[tile]

        h1, h2, h3 = _split3(lw)
        cum = _dot(lower, h1) + (_dot(lower, h2) + _dot(lower, h3))
        yield
        cum_c = cum[c - 1:c, :]
        a_t = a * jnp.exp(cum - lw)
        r_t = r * jnp.exp(cum)
        einv = jnp.exp(-cum)
        b_t = b * einv
        k_t = k * einv
        eend = jnp.exp(cum_c - cum)
        b_e = b * eend
        k_e = k * eend

        sc0 = _dot_hp(jnp.concatenate([a_t * m0, r_t * m0], axis=0), jnp.concatenate([b_t, k_t], axis=0), _dot_nt)
        sc1 = _dot_hp(jnp.concatenate([r_t * m1, a_t * m1], axis=0), jnp.concatenate([k_t, b_t], axis=0), _dot_nt)
        ar_s = _dot_hp(jnp.concatenate([a_t, r_t], axis=0), s0, _dot_nt)
        yield
        sc0 = jnp.where(keep0, sc0, 0.0)
        sc1 = jnp.where(keep1, sc1, 0.0)
        v0 = v * m0
        v1 = v * m1
        ak = jnp.where(left_half, sc1[c:], sc0[:c])
        rhs = ar_s[:c] + _dot_hp(ak, jnp.concatenate([v1, v0], axis=0))
        yield

        p = jnp.where(top & left, sc0, 0.0) + jnp.where((~top) & (~left), sc1, 0.0)
        x = jnp.concatenate([rhs * m0, rhs * m1], axis=0)
        for i in range(n_double):
            last = i == n_double - 1
            if last:
                x = x + _dot_hp(p, x)
            elif wide:
                px = _dot_hp(p, jnp.concatenate([p, x], axis=1))
                p = px[:, :c2]
                x = x + px[:, c2:]
            else:
                p, x = _dot_hp(p, p), x + _dot_hp(p, x)
            yield
        u0 = x[:c]
        u1 = x[c:]

        if wide:
            y_uv = _dot_hp(jnp.concatenate([sc0[c:], sc1[:c]], axis=1), jnp.concatenate([u0, v0, v1, u1], axis=0))
        else:
            y_uv = (_dot_hp(sc0[c:], jnp.concatenate([u0, v0], axis=0))
                    + _dot_hp(sc1[:c], jnp.concatenate([v1, u1], axis=0)))
        upd = _dot_hp(jnp.concatenate([u0 + u1, v], axis=0), jnp.concatenate([b_e, k_e], axis=0), _dot_tn)
        yield
        y_ref[0, :, lanes] = ar_s[c:] + y_uv
        s_scr[tile] = s0 * jnp.exp(cum_c) + jnp.where(same_head, upd, 0.0)

    _interleave(tile_stages(tile) for tile in range(tiles))

    @pl.when(ci == pl.num_programs(2) - 1)
    def _():
        sT_ref[0] = s_scr[...]


def _wkv(r, lw, kh, v, kk, kka, s0, *, c):
    bsz, t, d = r.shape
    n_pairs = d // LANES
    hp = HEADS_PER_TILE
    eye = jnp.eye(hp, dtype=F32)
    s0_bd = jnp.einsum('bphvk,hg->bphvgk', s0.reshape(bsz, n_pairs, hp, HEAD_DIM, HEAD_DIM), eye)
    s0_bd = s0_bd.reshape(bsz, n_pairs, LANES, LANES)
    tiles = WKV_TILES_PER_STEP
    seq = pl.BlockSpec((1, c, tiles * LANES), lambda b, p, i: (b, i, p))
    st = pl.BlockSpec((1, tiles, LANES, LANES), lambda b, p, i: (b, p, 0, 0))
    y, s_bd = pl.pallas_call(
        functools.partial(_wkv_kernel, c=c, tiles=tiles),
        grid=(bsz, n_pairs // tiles, t // c),
        in_specs=[seq] * 6 + [st],
        out_specs=[seq, st],
        out_shape=[jax.ShapeDtypeStruct((bsz, t, d), F32), jax.ShapeDtypeStruct(s0_bd.shape, F32)],
        scratch_shapes=[pltpu.VMEM((tiles, LANES, LANES), F32)],
        compiler_params=_params(("arbitrary", "arbitrary", "arbitrary")),
        name="wkv",
    )(r, lw, kh, v, kk, kka, s0_bd)
    s_bd = s_bd.reshape(bsz, n_pairs, hp, HEAD_DIM, hp, HEAD_DIM)
    s_fin = jnp.stack([s_bd[:, :, h, :, h, :] for h in range(hp)], axis=2)
    return y, s_fin.reshape(s0.shape)


def _tm_post_kernel(y_ref, r_ref, kh_ref, v_ref, g_ref, x_ref, gnw_ref, gnb_ref, rk_ref, wo_ref, out_ref):
    ones_bd = _head_ones()
    y = y_ref[...]
    mean = _head_sum(y, ones_bd) * (1.0 / HEAD_DIM)
    yc = y - mean
    var = _head_sum(yc * yc, ones_bd) * (1.0 / HEAD_DIM)
    yn = yc * lax.rsqrt(var + GN_EPS) * gnw_ref[...] + gnb_ref[...]
    bonus = _head_sum(r_ref[...] * kh_ref[...] * rk_ref[...], ones_bd) * v_ref[...]
    out_ref[...] = x_ref[...] + _dot(_bf((yn + bonus) * g_ref[...]), wo_ref[...])


def _tm_post(y, r, kh, v, g, x, w, *, tm):
    n, d = x.shape
    row = pl.BlockSpec((tm, d), lambda i: (i, 0))
    weights = [w['gn_w'], w['gn_b'], w['r_k'], w['wo']]
    return pl.pallas_call(
        _tm_post_kernel,
        grid=(n // tm,),
        in_specs=[row] * 6 + [_const_spec(a.shape) for a in weights],
        out_specs=row,
        out_shape=jax.ShapeDtypeStruct((n, d), F32),
        compiler_params=_params(("arbitrary",)),
        name="tm_post",
    )(y, r, kh, v, g, x, *weights)


def _ffn_kernel(h_ref, init_ref, gn_ref, wg_ref, wu_ref, cw_ref, cb_ref, wd_ref, gfin_ref,
                out_ref, conv_out, abuf, *, step, pad, tm, final_norm):
    t = pl.program_id(1)
    hist = (CONV_W - 1) * step
    h = h_ref[0]
    hb = _bf(_rms(h, gn_ref[...]))

    @pl.when(t == 0)
    def _():
        abuf[pad - hist:pad, :] = init_ref[0]

    @pl.when(t > 0)
    def _():
        abuf[pad - hist:pad, :] = abuf[pad + tm - hist:pad + tm, :]

    a = _dot(hb, wg_ref[...])
    abuf[pad:pad + tm, :] = a
    cw = cw_ref[...]
    conv = cb_ref[...] + a * cw[CONV_W - 1:CONV_W, :]
    for i in range(CONV_W - 1):
        off = pad - (CONV_W - 1 - i) * step
        conv = conv + abuf[off:off + tm, :] * cw[i:i + 1, :]
    u = _dot(hb, wu_ref[...])
    act = conv * _sigmoid(conv) * u
    out = h + _dot(_bf(act), wd_ref[...])
    if final_norm:
        out = _rms(out, gfin_ref[...])
    out_ref[0] = out
    conv_out[0] = abuf[pad + tm - hist:pad + tm, :]


def _ffn(h, init, w, gfin, *, step, tm, final_norm):
    nb, tt, d = h.shape
    f = w['wg'].shape[1]
    hist = (CONV_W - 1) * step
    pad = max(SUBLANES, hist)
    row = pl.BlockSpec((1, tm, d), lambda b, t: (b, t, 0))
    cst = pl.BlockSpec((1, hist, f), lambda b, t: (b, 0, 0))
    weights = [w['gn'], w['wg'], w['wu'], w['cw'], w['cb'], w['wd'], gfin]
    return pl.pallas_call(
        functools.partial(_ffn_kernel, step=step, pad=pad, tm=tm, final_norm=final_norm),
        grid=(nb, tt // tm),
        in_specs=[row, cst] + [_const_spec(a.shape) for a in weights],
        out_specs=[row, cst],
        out_shape=[jax.ShapeDtypeStruct((nb, tt, d), F32), jax.ShapeDtypeStruct((nb, hist, f), F32)],
        scratch_shapes=[pltpu.VMEM((pad + tm, f), F32)],
        compiler_params=_params(("arbitrary", "arbitrary")),
        name="ffn_final" if final_norm else "ffn",
    )(h, init, *weights)


def _qkv_kernel(h_ref, gn_ref, w_ref, b_ref, q_out, k_out, v_out):
    d = h_ref.shape[-1]
    hb = _bf(_rms(h_ref[...], gn_ref[...]))
    qkv = _dot(hb, w_ref[...]) + b_ref[...]
    q_out[...] = qkv[:, :d]
    k_out[...] = qkv[:, d:2 * d]
    v_out[...] = qkv[:, 2 * d:]


def _qkv(h, w, *, tm):
    n, d = h.shape
    row = pl.BlockSpec((tm, d), lambda i: (i, 0))
    weights = [w['gn'], w['wqkv'], w['bqkv']]
    return pl.pallas_call(
        _qkv_kernel,
        grid=(n // tm,),
        in_specs=[row] + [_const_spec(a.shape) for a in weights],
        out_specs=[row] * 3,
        out_shape=[jax.ShapeDtypeStruct((n, d), F32)] * 3,
        compiler_params=_params(("arbitrary",)),
        name="sb_qkv",
    )(h, *weights)


def _qkv_kt_kernel(h_ref, gn_ref, w_ref, b_ref, wkt_ref, bk_ref, q_out, kt_out, v_out):
    d = h_ref.shape[-1]
    hb = _bf(_rms(h_ref[0], gn_ref[...]))
    qv = _dot(hb, w_ref[...]) + b_ref[...]
    q_out[0] = qv[:, :d]
    v_out[0] = qv[:, d:]
    kt_out[0] = _dot_nt(wkt_ref[...], hb) + bk_ref[...]


def _qkv_kt(h, w, *, tm):
    bsz, t, d = h.shape
    row = pl.BlockSpec((1, tm, d), lambda b, i: (b, i, 0))
    col = pl.BlockSpec((1, d, tm), lambda b, i: (b, 0, i))
    weights = [w['gn'], w['wqv'], w['bqv'], w['wkt'], w['bk_col']]
    return pl.pallas_call(
        _qkv_kt_kernel,
        grid=(bsz, t // tm),
        in_specs=[row] + [_const_spec(a.shape) for a in weights],
        out_specs=[row, col, row],
        out_shape=[jax.ShapeDtypeStruct((bsz, t, d), F32), jax.ShapeDtypeStruct((bsz, d, t), F32),
                   jax.ShapeDtypeStruct((bsz, t, d), F32)],
        compiler_params=_params(("arbitrary", "arbitrary")),
        name="sb_qkv_kt",
    )(h, *weights)


def _suffix_ones(n):
    r = lax.broadcasted_iota(jnp.int32, (n, 2 * n), 0)
    c = lax.broadcasted_iota(jnp.int32, (n, 2 * n), 1)
    return _indicator((r >= c) | (c >= n), BF16)


def _stick_stages(qm, kb, vb, valid, tri, run, keys_minor=False):
    nk = tri.shape[0]
    z = _dot(qm, kb) if keys_minor else _dot_nt(qm, kb)
    yield
    lb = _log_sigmoid(z)
    lk = lb - z
    if valid is not None:
        lk = jnp.where(valid, lk, 0.0)
    cs = _dot_sel(lk, tri)
    yield
    la = cs[:, :nk] - lk + run['carry']
    run['carry'] = run['carry'] + cs[:, nk:]
    att = jnp.exp(lb + la)
    if valid is not None:
        att = jnp.where(valid, att, 0.0)
    pv = _dot_nt(_bf(att), vb) if keys_minor else _dot(_bf(att), vb)
    yield
    run['acc'] = run['acc'] + pv


def _attn_kernel(q_ref, kt_ref, v_ref, o_ref, acc_ref, carry_ref, *, qb, kg):
    nsub = kg // qb
    t = q_ref.shape[1]
    lane_head = lax.broadcasted_iota(jnp.int32, (1, LANES), 1) // HEAD_DIM
    tri = _suffix_ones(qb)

    def group(qms, start, nblk, diag):
        width = nblk * qb
        ktg = _bf(kt_ref[0, :, pl.ds(start, width)])
        vg = _bf(v_ref[0, pl.ds(start, width), :])
        if diag:
            row = lax.broadcasted_iota(jnp.int32, (qb, width), 0)
            col = lax.broadcasted_iota(jnp.int32, (qb, width), 1)
            valid = col < row + (nblk - 1) * qb

        def head_stages(h):
            z = _dot(qms[h], ktg)
            yield
            lb = _log_sigmoid(z)
            lk = lb - z
            if diag:
                lk = jnp.where(valid, lk, 0.0)
            hi, lo = _split2(lk)
            subs = [slice(s * qb, (s + 1) * qb) for s in range(nblk)]
            cs = [_dot(hi[:, sub], tri) + _dot(lo[:, sub], tri) for sub in subs]
            yield
            run = carry_ref[h]
            las = [None] * nblk
            for s in reversed(range(nblk)):
                las[s] = cs[s][:, :qb] - lk[:, subs[s]] + run
                run = run + cs[s][:, qb:]
            carry_ref[h] = run
            att = jnp.exp(lb + jnp.concatenate(las, axis=1))
            if diag:
                att = jnp.where(valid, att, 0.0)
            pv = _dot(_bf(att), vg)
            yield
            acc_ref[h] += pv

        _interleave(head_stages(h) for h in range(HEADS_PER_TILE))

    def key_group_of_queries(gi, carry):
        for r in range(nsub):
            q_start = pl.multiple_of(gi * kg + r * qb, qb)
            q = q_ref[0, pl.ds(q_start, qb), :] * (HEAD_DIM ** -0.5)
            qms = [_bf(jnp.where(lane_head == h, q, 0.0)) for h in range(HEADS_PER_TILE)]
            acc_ref[...] = jnp.zeros(acc_ref.shape, F32)
            carry_ref[...] = jnp.zeros(carry_ref.shape, F32)
            group(qms, pl.multiple_of(gi * kg, kg), r + 1, True)

            def earlier(jj, c):
                group(qms, pl.multiple_of((gi - 1 - jj) * kg, kg), nsub, False)
                return c

            lax.fori_loop(0, gi, earlier, 0)
            o_ref[0, pl.ds(q_start, qb), :] = jnp.where(lane_head == 0, acc_ref[0], acc_ref[1])
        return carry

    lax.fori_loop(0, t // kg, key_group_of_queries, 0)


def _attn_prompt(q, kt, v, *, qb, kg):
    bsz, t, d = q.shape
    n_pairs = d // LANES
    seq = pl.BlockSpec((1, t, LANES), lambda b, p: (b, 0, p))
    ks = pl.BlockSpec((1, LANES, t), lambda b, p: (b, p, 0))
    return pl.pallas_call(
        functools.partial(_attn_kernel, qb=qb, kg=kg),
        grid=(bsz, n_pairs),
        in_specs=[seq, ks, seq],
        out_specs=seq,
        out_shape=jax.ShapeDtypeStruct((bsz, t, d), F32),
        scratch_shapes=[pltpu.VMEM((HEADS_PER_TILE, qb, LANES), F32), pltpu.VMEM((HEADS_PER_TILE, qb, qb), F32)],
        compiler_params=_params(("arbitrary", "arbitrary")),
        name="sb_attn",
    )(q, kt, v)


def _paged_kernel(pt_ref, q_ref, kn_ref, vn_ref, *refs, tq, n_heads, page, group):
    kc_refs = refs[:group]
    vc_refs = refs[group:2 * group]
    o_ref, qexp, acc_ref, carry_ref, kpad, vpad = refs[2 * group:]
    p = pl.program_id(1)
    d = q_ref.shape[-1]
    nq = tq * n_heads
    tri = _suffix_ones(page)
    head_of_lane = lax.broadcasted_iota(jnp.int32, (n_heads, d), 1) // HEAD_DIM
    head_mask = head_of_lane == lax.broadcasted_iota(jnp.int32, (n_heads, d), 0)

    @pl.when(p == 0)
    def _():
        q = q_ref[0] * (HEAD_DIM ** -0.5)
        for t in range(tq):
            qexp[t * n_heads:(t + 1) * n_heads, :] = _bf(jnp.where(head_mask, q[t:t + 1, :], 0.0))
        kpad[...] = jnp.zeros(kpad.shape, F32)
        vpad[...] = jnp.zeros(vpad.shape, F32)
        kpad[0:tq, :] = kn_ref[0]
        vpad[0:tq, :] = vn_ref[0]
        row_t = lax.broadcasted_iota(jnp.int32, (nq, page), 0) // n_heads
        col = lax.broadcasted_iota(jnp.int32, (nq, page), 1)
        run = dict(carry=jnp.zeros((nq, page), F32), acc=jnp.zeros((nq, d), F32))
        _interleave([_stick_stages(qexp[...], _bf(kpad[...]), _bf(vpad[...]), col < row_t, tri, run)])
        acc_ref[...] = run['acc']
        carry_ref[...] = run['carry']

    run = dict(carry=carry_ref[...], acc=acc_ref[...])
    _interleave(_stick_stages(qexp[...], _bf(kc_ref[0]), _bf(vc_ref[0]), None, tri, run, keys_minor=True)
                for kc_ref, vc_ref in zip(kc_refs, vc_refs))
    acc = run['acc']
    acc_ref[...] = acc
    carry_ref[...] = run['carry']

    @pl.when(p == pl.num_programs(1) - 1)
    def _():
        rows = [jnp.sum(jnp.where(head_mask, acc[t * n_heads:(t + 1) * n_heads, :], 0.0), axis=0, keepdims=True)
                for t in range(tq)]
        o_ref[0] = jnp.concatenate(rows, axis=0)


def _attn_paged(q, k_new, v_new, cache_k, cache_v, page_table):
    bsz, tq, d = q.shape
    n_heads = d // HEAD_DIM
    n_pages = page_table.shape[1]
    page = cache_k.shape[2]
    nq = tq * n_heads
    group = math.gcd(PAGES_PER_STEP, n_pages)
    new = pl.BlockSpec((1, tq, d), lambda b, p, pt: (b, 0, 0))
    pgs = [pl.BlockSpec((1, d, page), lambda b, p, pt, g=g: (pt[b, n_pages - 1 - (p * group + g)], 0, 0))
           for g in range(group)]
    grid_spec = pltpu.PrefetchScalarGridSpec(
        num_scalar_prefetch=1,
        grid=(bsz, n_pages // group),
        in_specs=[new, new, new] + pgs + pgs,
        out_specs=new,
        scratch_shapes=[pltpu.VMEM((nq, d), BF16), pltpu.VMEM((nq, d), F32), pltpu.VMEM((nq, page), F32),
                        pltpu.VMEM((page, d), F32), pltpu.VMEM((page, d), F32)],
    )
    return pl.pallas_call(
        functools.partial(_paged_kernel, tq=tq, n_heads=n_heads, page=page, group=group),
        grid_spec=grid_spec,
        out_shape=jax.ShapeDtypeStruct((bsz, tq, d), F32),
        compiler_params=_params(("arbitrary", "arbitrary")),
        name="sb_paged",
    )(page_table, q, k_new, v_new, *([cache_k] * group), *([cache_v] * group))


def _proj_res_kernel(x_ref, res_ref, w_ref, out_ref):
    out_ref[...] = res_ref[...] + _dot(_bf(x_ref[...]), w_ref[...])


def _proj_res(x, res, w, *, tm):
    n, d = x.shape
    row = pl.BlockSpec((tm, d), lambda i: (i, 0))
    return pl.pallas_call(
        _proj_res_kernel,
        grid=(n // tm,),
        in_specs=[row, row, _const_spec(w.shape)],
        out_specs=row,
        out_shape=jax.ShapeDtypeStruct((n, d), F32),
        compiler_params=_params(("arbitrary",)),
        name="sb_out",
    )(x, res, w)


def _weights(norm_mix, norm_ffn, norm_final, tm_mu, tm_w_r, tm_w_k, tm_w_v, tm_w_o, tm_w0, tm_w_w1, tm_w_w2,
             tm_a0, tm_w_a1, tm_w_a2, tm_w_g1, tm_w_g2, tm_k_k, tm_k_a, tm_r_k, tm_gn_w, tm_gn_b,
             sb_w_qkv, sb_b_qkv, sb_w_o, ffn_w_gate, ffn_w_up, ffn_conv_w, ffn_conv_b, ffn_w_down):
    row = lambda a: a.reshape(1, -1)
    tm = dict(gmix=row(norm_mix[0]), mu=tm_mu[0], wr=_bf(tm_w_r[0]), wk=_bf(tm_w_k[0]), wv=_bf(tm_w_v[0]),
              w1=_bf(tm_w_w1[0]), w2=_bf(tm_w_w2[0]), w0=row(tm_w0[0]), a1=_bf(tm_w_a1[0]), a2=_bf(tm_w_a2[0]),
              a0=row(tm_a0[0]), g1=_bf(tm_w_g1[0]), g2=_bf(tm_w_g2[0]), k_k=row(tm_k_k[0]), k_a=row(tm_k_a[0]),
              gn_w=row(tm_gn_w[0]), gn_b=row(tm_gn_b[0]), r_k=row(tm_r_k[0]), wo=_bf(tm_w_o[0]))
    d = norm_final.shape[0]
    wqkv, bqkv = sb_w_qkv[0], sb_b_qkv[0]
    qv_cols = lambda a: jnp.concatenate([a[..., :d], a[..., 2 * d:]], axis=-1)
    sb = dict(gn=row(norm_mix[1]), wqkv=_bf(wqkv), bqkv=row(bqkv), wo=_bf(sb_w_o[0]),
              wqv=_bf(qv_cols(wqkv)), bqv=row(qv_cols(bqkv)), wkt=_bf(wqkv[:, d:2 * d].T),
              bk_col=bqkv[d:2 * d].reshape(d, 1))
    ffn = [dict(gn=row(norm_ffn[i]), wg=_bf(ffn_w_gate[i]), wu=_bf(ffn_w_up[i]), cw=ffn_conv_w[i],
                cb=row(ffn_conv_b[i]), wd=_bf(ffn_w_down[i])) for i in range(2)]
    return tm, sb, ffn, row(norm_final)


def _trunk_prompt(x, w_tm, w_sb, w_ffn, g_final, *, tm, chunk, qb, kg):
    bsz, t, d = x.shape
    n = bsz * t
    f = w_ffn[0]['wg'].shape[1]
    flat = lambda a: a.reshape(n, d)
    r, lw, kh, v, kk, kka, g, shift = _tm_pre(x, jnp.zeros((bsz, 1, d), F32), w_tm, step=1, tm=tm)
    s0 = jnp.zeros((bsz, d // HEAD_DIM, HEAD_DIM, HEAD_DIM), F32)
    y, wkv = _wkv(r, lw, kh, v, kk, kka, s0, c=chunk)
    h = _tm_post(flat(y), flat(r), flat(kh), flat(v), flat(g), flat(x), w_tm, tm=tm)
    conv0 = jnp.zeros((bsz, CONV_W - 1, f), F32)
    h, conv_a = _ffn(h.reshape(bsz, t, d), conv0, w_ffn[0], g_final, step=1, tm=tm, final_norm=False)
    q, kt, v2 = _qkv_kt(h, w_sb, tm=tm)
    o = _attn_prompt(q, kt, v2, qb=qb, kg=kg)
    h = _proj_res(flat(o), flat(h), w_sb['wo'], tm=tm)
    out, conv_b = _ffn(h.reshape(bsz, t, d), conv0, w_ffn[1], g_final, step=1, tm=tm, final_norm=True)
    n_heads = d // HEAD_DIM
    k_new = jnp.transpose(kt.reshape(bsz, n_heads, HEAD_DIM, t), (0, 3, 1, 2))
    return (out, shift.reshape(1, bsz, d), wkv[None], k_new[None], v2.reshape(bsz, t, n_heads, HEAD_DIM)[None],
            jnp.stack([conv_a, conv_b]))


def _trunk_sample(x, state_shift, state_wkv, state_conv, cache_k, cache_v, page_table, w_tm, w_sb, w_ffn, g_final,
                  *, chunk):
    bsz, t, d = x.shape
    n = bsz * t
    f = w_ffn[0]['wg'].shape[1]
    to_tm = lambda a: jnp.swapaxes(a, 0, 1).reshape(1, n, a.shape[-1])
    to_bm = lambda a: jnp.swapaxes(a.reshape(t, bsz, a.shape[-1]), 0, 1)
    xs = to_tm(x)
    r, lw, kh, v, kk, kka, g, shift = _tm_pre(xs, state_shift[0][None], w_tm, step=bsz, tm=n)
    pad_t = lambda a: jnp.pad(to_bm(a), ((0, 0), (0, chunk - t), (0, 0)))
    y, wkv = _wkv(pad_t(r), pad_t(lw), pad_t(kh), pad_t(v), pad_t(kk), pad_t(kka), state_wkv[0], c=chunk)
    y = to_tm(y[:, :t])
    flat = lambda a: a.reshape(n, d)
    h = _tm_post(flat(y), flat(r), flat(kh), flat(v), flat(g), flat(xs), w_tm, tm=n)
    conv_tm = lambda a: jnp.swapaxes(a, 0, 1).reshape(1, (CONV_W - 1) * bsz, f)
    conv_bm = lambda a: jnp.swapaxes(a.reshape(CONV_W - 1, bsz, f), 0, 1)
    h, conv_a = _ffn(h.reshape(1, n, d), conv_tm(state_conv[0]), w_ffn[0], g_final, step=bsz, tm=n,
                     final_norm=False)
    q, k, v2 = _qkv(flat(h), w_sb, tm=n)
    q, k, v2 = to_bm(q), to_bm(k), to_bm(v2)
    n_phys, page = cache_k.shape[1], cache_k.shape[2]
    keys_minor = lambda c: jnp.transpose(c[0], (0, 2, 3, 1)).reshape(n_phys, d, page)
    o = _attn_paged(q, k, v2, keys_minor(cache_k), keys_minor(cache_v), page_table)
    h = _proj_res(flat(to_tm(o)), flat(h), w_sb['wo'], tm=n)
    out, conv_b = _ffn(h.reshape(1, n, d), conv_tm(state_conv[1]), w_ffn[1], g_final, step=bsz, tm=n,
                       final_norm=True)
    hd = (bsz, t, d // HEAD_DIM, HEAD_DIM)
    return (to_bm(out[0]), shift.reshape(1, bsz, d), wkv[None], k.reshape(hd)[None], v2.reshape(hd)[None],
            jnp.stack([conv_bm(conv_a[0]), conv_bm(conv_b[0])]))


PROMPT_TILE = 256
WKV_TILES_PER_STEP = 4
WKV_CHUNK = 64
SAMPLE_WKV_CHUNK = 8
Q_BLOCK = 128
KEY_GROUP = 512
PAGES_PER_STEP = 4


def kernel(x_prompt, x_sample, state_shift, state_wkv, cache_k, cache_v, state_conv, page_table, norm_mix, norm_ffn, norm_final, tm_mu, tm_w_r, tm_w_k, tm_w_v, tm_w_o, tm_w0, tm_w_w1, tm_w_w2, tm_a0, tm_w_a1, tm_w_a2, tm_w_g1, tm_w_g2, tm_k_k, tm_k_a, tm_r_k, tm_gn_w, tm_gn_b, sb_w_qkv, sb_b_qkv, sb_w_o, ffn_w_gate, ffn_w_up, ffn_conv_w, ffn_conv_b, ffn_w_down):
    w_tm, w_sb, w_ffn, g_final = _weights(
        norm_mix, norm_ffn, norm_final, tm_mu, tm_w_r, tm_w_k, tm_w_v, tm_w_o, tm_w0, tm_w_w1, tm_w_w2,
        tm_a0, tm_w_a1, tm_w_a2, tm_w_g1, tm_w_g2, tm_k_k, tm_k_a, tm_r_k, tm_gn_w, tm_gn_b,
        sb_w_qkv, sb_b_qkv, sb_w_o, ffn_w_gate, ffn_w_up, ffn_conv_w, ffn_conv_b, ffn_w_down)
    tile = min(PROMPT_TILE, x_prompt.shape[1])
    y_p, shift_p, wkv_p, k_p, v_p, conv_p = _trunk_prompt(
        x_prompt, w_tm, w_sb, w_ffn, g_final, tm=tile, chunk=min(WKV_CHUNK, x_prompt.shape[1]),
        qb=min(Q_BLOCK, x_prompt.shape[1]), kg=min(KEY_GROUP, x_prompt.shape[1]))
    y_s, shift_s, wkv_s, k_s, v_s, conv_s = _trunk_sample(
        x_sample, state_shift, state_wkv, state_conv, cache_k, cache_v, page_table, w_tm, w_sb, w_ffn, g_final,
        chunk=SAMPLE_WKV_CHUNK)
    return (y_p, y_s, shift_p, wkv_p, k_p, v_p, conv_p, shift_s, wkv_s, k_s, v_s, conv_s)
```

```python
import functools
import math

import jax
import jax.numpy as jnp
from jax import lax
from jax.experimental import pallas as pl
from jax.experimental.pallas import tpu as pltpu

HEAD_DIM = 64
LANES = 128
SUBLANES = 8
HEADS_PER_TILE = LANES // HEAD_DIM
CONV_W = 3
RMS_EPS = 1e-6
GN_EPS = 64e-5
KK_EPS = 1e-12
N_MIX = 6
VMEM_LIMIT = 56 * 1024 * 1024

F32 = jnp.float32
BF16 = jnp.bfloat16


def _bf(x):
    return x.astype(BF16)


def _indicator(mask, dtype=F32):
    return jnp.where(mask, 1.0, 0.0).astype(dtype)


def _dot(a, b):
    return jnp.dot(a, b, preferred_element_type=F32)


def _dot_nt(a, b):
    return lax.dot_general(a, b, (((1,), (1,)), ((), ())), preferred_element_type=F32)


def _dot_tn(a, b):
    return lax.dot_general(a, b, (((0,), (0,)), ((), ())), preferred_element_type=F32)


def _split2(x):
    hi = x.astype(BF16)
    lo = (x - hi.astype(F32)).astype(BF16)
    return hi, lo


def _split3(x):
    hi = x.astype(BF16)
    r1 = x - hi.astype(F32)
    mid = r1.astype(BF16)
    lo = (r1 - mid.astype(F32)).astype(BF16)
    return hi, mid, lo


def _dot_sel(x, m):
    hi, lo = _split2(x)
    return _dot(hi, m) + _dot(lo, m)


def _dot_hp(a, b, dot=_dot):
    ah, al = _split2(a)
    bh, bl = _split2(b)
    return dot(ah, bh) + (dot(ah, bl) + dot(al, bh))


def _rms(x, g):
    return x * lax.rsqrt(jnp.mean(x * x, axis=-1, keepdims=True) + RMS_EPS) * g


def _softplus(x):
    return jnp.maximum(x, 0.0) + jnp.log(1.0 + jnp.exp(-jnp.abs(x)))


def _sigmoid(x):
    return 1.0 / (1.0 + jnp.exp(-x))


def _log_sigmoid(x):
    return -_softplus(-x)


def _head_ones():
    r = lax.broadcasted_iota(jnp.int32, (LANES, LANES), 0) // HEAD_DIM
    c = lax.broadcasted_iota(jnp.int32, (LANES, LANES), 1) // HEAD_DIM
    return _indicator(r == c, BF16)


def _head_sum(x, ones_bd):
    d = x.shape[-1]
    cols = [_dot_sel(x[:, g * LANES:(g + 1) * LANES], ones_bd) for g in range(d // LANES)]
    return jnp.concatenate(cols, axis=1)


def _interleave(stage_generators):
    live = list(stage_generators)
    done = object()
    while live:
        live = [g for g in live if next(g, done) is not done]


def _const_spec(shape):
    nd = len(shape)
    return pl.BlockSpec(shape, lambda *_: (0,) * nd, pipeline_mode=pl.Buffered(1))


def _params(sem):
    return pltpu.CompilerParams(dimension_semantics=sem, vmem_limit_bytes=VMEM_LIMIT)


def _tm_pre_kernel(x_ref, init_ref, gmix_ref, mu_ref, wr_ref, wk_ref, wv_ref, w1_ref, w2_ref, w0_ref,
                   a1_ref, a2_ref, a0_ref, g1_ref, g2_ref, kk_ref, ka_ref,
                   r_out, lw_out, kh_out, v_out, kk_out, kka_out, g_out, shift_out, hbuf, *, step, pad, tm):
    t = pl.program_id(1)
    hn = _rms(x_ref[0], gmix_ref[...])

    @pl.when(t == 0)
    def _():
        hbuf[pad - step:pad, :] = init_ref[0]

    @pl.when(t > 0)
    def _():
        hbuf[pad - step:pad, :] = hbuf[pad + tm - step:pad + tm, :]

    hbuf[pad:pad + tm, :] = hn
    xx = hbuf[pad - step:pad - step + tm, :] - hn
    mu = mu_ref[...]
    xr, xw, xk, xv, xa, xg = [_bf(hn + xx * mu[i:i + 1, :]) for i in range(N_MIX)]
    r = _dot(xr, wr_ref[...])
    k = _dot(xk, wk_ref[...])
    v = _dot(xv, wv_ref[...])
    wl = w0_ref[...] + _dot(_bf(jnp.tanh(_dot(xw, w1_ref[...]))), w2_ref[...])
    w_log = -_softplus(-wl) - 0.5
    a = _sigmoid(a0_ref[...] + _dot(_bf(_dot(xa, a1_ref[...])), a2_ref[...]))
    g = _dot(_bf(_sigmoid(_dot(xg, g1_ref[...]))), g2_ref[...])
    kk = k * kk_ref[...]
    nrm = jnp.sqrt(_head_sum(kk * kk, _head_ones()))
    kk = kk / jnp.maximum(nrm, KK_EPS)
    r_out[0] = r
    lw_out[0] = -jnp.exp(w_log)
    kh_out[0] = k * (1.0 + (a - 1.0) * ka_ref[...])
    v_out[0] = v
    kk_out[0] = kk
    kka_out[0] = kk * a
    g_out[0] = g
    shift_out[0] = hn[tm - step:, :]


def _tm_pre(x, init, w, *, step, tm):
    nb, tt, d = x.shape
    pad = max(SUBLANES, step)
    nt = tt // tm
    row = pl.BlockSpec((1, tm, d), lambda b, t: (b, t, 0))
    weights = [w['gmix'], w['mu'], w['wr'], w['wk'], w['wv'], w['w1'], w['w2'], w['w0'],
               w['a1'], w['a2'], w['a0'], w['g1'], w['g2'], w['k_k'], w['k_a']]
    outs = pl.pallas_call(
        functools.partial(_tm_pre_kernel, step=step, pad=pad, tm=tm),
        grid=(nb, nt),
        in_specs=[row, pl.BlockSpec((1, step, d), lambda b, t: (b, 0, 0))] + [_const_spec(a.shape) for a in weights],
        out_specs=[row] * 7 + [pl.BlockSpec((1, step, d), lambda b, t: (b, 0, 0))],
        out_shape=[jax.ShapeDtypeStruct((nb, tt, d), F32)] * 7 + [jax.ShapeDtypeStruct((nb, step, d), F32)],
        scratch_shapes=[pltpu.VMEM((pad + tm, d), F32)],
        compiler_params=_params(("arbitrary", "arbitrary")),
        name="tm_pre",
    )(x, init, *weights)
    return outs


def _wkv_kernel(r_ref, lw_ref, k_ref, v_ref, kk_ref, kka_ref, s0_ref, y_ref, sT_ref, s_scr, *, c, tiles):
    ci = pl.program_id(2)
    c2 = 2 * c
    lane_head = lax.broadcasted_iota(jnp.int32, (1, LANES), 1) // HEAD_DIM
    m0 = _indicator(lane_head == 0)
    m1 = 1.0 - m0
    hr = lax.broadcasted_iota(jnp.int32, (LANES, LANES), 0) // HEAD_DIM
    hc = lax.broadcasted_iota(jnp.int32, (LANES, LANES), 1) // HEAD_DIM
    same_head = hr == hc
    trow = lax.broadcasted_iota(jnp.int32, (c, c), 0)
    tcol = lax.broadcasted_iota(jnp.int32, (c, c), 1)
    lower = _indicator(tcol <= trow, BF16)
    rr = lax.broadcasted_iota(jnp.int32, (c2, c2), 0)
    cc = lax.broadcasted_iota(jnp.int32, (c2, c2), 1)
    ti = rr % c
    tj = cc % c
    top = rr < c
    left = cc < c
    keep0 = tj < ti + jnp.where(top, 0, 1)
    keep1 = tj < ti + jnp.where(top, 1, 0)
    left_half = lax.broadcasted_iota(jnp.int32, (c, c2), 1) < c
    wide = c2 % LANES == 0
    eye = _indicator(rr == cc)
    base = min(WKV_INVERSE_BASE, c)
    n_base = int(math.log2(base))

    @pl.when(ci == 0)
    def _():
        s_scr[...] = s0_ref[0]

    def tile_stages(tile):
        lanes = slice(tile * LANES, (tile + 1) * LANES)
        r = r_ref[0, :, lanes]
        lw = lw_ref[0, :, lanes]
        k = k_ref[0, :, lanes]
        v = v_ref[0, :, lanes]
        a = -kk_ref[0, :, lanes]
        b = kka_ref[0, :, lanes]
        s0 = s_scr[tile]

        h1, h2, h3 = _split3(lw)
        cum = _dot(lower, h1) + (_dot(lower, h2) + _dot(lower, h3))
        yield
        cum_c = cum[c - 1:c, :]
        a_t = a * jnp.exp(cum - lw)
        r_t = r * jnp.exp(cum)
        einv = jnp.exp(-cum)
        b_t = b * einv
        k_t = k * einv
        eend = jnp.exp(cum_c - cum)
        b_e = b * eend
        k_e = k * eend

        sc0 = _dot_hp(jnp.concatenate([a_t * m0, r_t * m0], axis=0), jnp.concatenate([b_t, k_t], axis=0), _dot_nt)
        sc1 = _dot_hp(jnp.concatenate([r_t * m1, a_t * m1], axis=0), jnp.concatenate([k_t, b_t], axis=0), _dot_nt)
        ar_s = _dot_hp(jnp.concatenate([a_t, r_t], axis=0), s0, _dot_nt)
        yield
        sc0 = jnp.where(keep0, sc0, 0.0)
        sc1 = jnp.where(keep1, sc1, 0.0)
        v0 = v * m0
        v1 = v * m1
        ak = jnp.where(left_half, sc1[c:], sc0[:c])
        rhs = ar_s[:c] + _dot_hp(ak, jnp.concatenate([v1, v0], axis=0))
        yield

        n_bd = jnp.where(top & left, sc0, 0.0) + jnp.where((~top) & (~left), sc1, 0.0)
        p = jnp.where(ti // base == tj // base, n_bd, 0.0)
        t_inv = eye + p
        for _ in range(1, n_base):
            p = _dot_hp(p, p)
            yield
            t_inv = t_inv + _dot_hp(p, t_inv)
            yield
        m = base
        while m < c:
            n_off = jnp.where((ti // (2 * m) == tj // (2 * m)) & (ti // m != tj // m), n_bd, 0.0)
            nt = _dot_hp(n_off, t_inv)
            yield
            t_inv = t_inv + _dot_hp(t_inv, nt)
            yield
            m *= 2
        x = _dot_hp(t_inv, jnp.concatenate([rhs * m0, rhs * m1], axis=0))
        yield
        u0 = x[:c]
        u1 = x[c:]

        if wide:
            y_uv = _dot_hp(jnp.concatenate([sc0[c:], sc1[:c]], axis=1), jnp.concatenate([u0, v0, v1, u1], axis=0))
        else:
            y_uv = (_dot_hp(sc0[c:], jnp.concatenate([u0, v0], axis=0))
                    + _dot_hp(sc1[:c], jnp.concatenate([v1, u1], axis=0)))
        upd = _dot_hp(jnp.concatenate([u0 + u1, v], axis=0), jnp.concatenate([b_e, k_e], axis=0), _dot_tn)
        yield
        y_ref[0, :, lanes] = ar_s[c:] + y_uv
        s_scr[tile] = s0 * jnp.exp(cum_c) + jnp.where(same_head, upd, 0.0)

    _interleave(tile_stages(tile) for tile in range(tiles))

    @pl.when(ci == pl.num_programs(2) - 1)
    def _():
        sT_ref[0] = s_scr[...]


def _wkv(r, lw, kh, v, kk, kka, s0, *, c):
    bsz, t, d = r.shape
    n_pairs = d // LANES
    hp = HEADS_PER_TILE
    eye = jnp.eye(hp, dtype=F32)
    s0_bd = jnp.einsum('bphvk,hg->bphvgk', s0.reshape(bsz, n_pairs, hp, HEAD_DIM, HEAD_DIM), eye)
    s0_bd = s0_bd.reshape(bsz, n_pairs, LANES, LANES)
    tiles = WKV_TILES_PER_STEP
    seq = pl.BlockSpec((1, c, tiles * LANES), lambda b, p, i: (b, i, p))
    st = pl.BlockSpec((1, tiles, LANES, LANES), lambda b, p, i: (b, p, 0, 0))
    y, s_bd = pl.pallas_call(
        functools.partial(_wkv_kernel, c=c, tiles=tiles),
        grid=(bsz, n_pairs // tiles, t // c),
        in_specs=[seq] * 6 + [st],
        out_specs=[seq, st],
        out_shape=[jax.ShapeDtypeStruct((bsz, t, d), F32), jax.ShapeDtypeStruct(s0_bd.shape, F32)],
        scratch_shapes=[pltpu.VMEM((tiles, LANES, LANES), F32)],
        compiler_params=_params(("arbitrary", "arbitrary", "arbitrary")),
        name="wkv",
    )(r, lw, kh, v, kk, kka, s0_bd)
    s_bd = s_bd.reshape(bsz, n_pairs, hp, HEAD_DIM, hp, HEAD_DIM)
    s_fin = jnp.stack([s_bd[:, :, h, :, h, :] for h in range(hp)], axis=2)
    return y, s_fin.reshape(s0.shape)


def _tm_post_kernel(y_ref, r_ref, kh_ref, v_ref, g_ref, x_ref, gnw_ref, gnb_ref, rk_ref, wo_ref, out_ref):
    ones_bd = _head_ones()
    y = y_ref[...]
    mean = _head_sum(y, ones_bd) * (1.0 / HEAD_DIM)
    yc = y - mean
    var = _head_sum(yc * yc, ones_bd) * (1.0 / HEAD_DIM)
    yn = yc * lax.rsqrt(var + GN_EPS) * gnw_ref[...] + gnb_ref[...]
    bonus = _head_sum(r_ref[...] * kh_ref[...] * rk_ref[...], ones_bd) * v_ref[...]
    out_ref[...] = x_ref[...] + _dot(_bf((yn + bonus) * g_ref[...]), wo_ref[...])


def _tm_post(y, r, kh, v, g, x, w, *, tm):
    n, d = x.shape
    row = pl.BlockSpec((tm, d), lambda i: (i, 0))
    weights = [w['gn_w'], w['gn_b'], w['r_k'], w['wo']]
    return pl.pallas_call(
        _tm_post_kernel,
        grid=(n // tm,),
        in_specs=[row] * 6 + [_const_spec(a.shape) for a in weights],
        out_specs=row,
        out_shape=jax.ShapeDtypeStruct((n, d), F32),
        compiler_params=_params(("arbitrary",)),
        name="tm_post",
    )(y, r, kh, v, g, x, *weights)


def _ffn_kernel(h_ref, init_ref, gn_ref, wg_ref, wu_ref, cw_ref, cb_ref, wd_ref, gfin_ref,
                out_ref, conv_out, abuf, *, step, pad, tm, final_norm):
    t = pl.program_id(1)
    hist = (CONV_W - 1) * step
    h = h_ref[0]
    hb = _bf(_rms(h, gn_ref[...]))

    @pl.when(t == 0)
    def _():
        abuf[pad - hist:pad, :] = init_ref[0]

    @pl.when(t > 0)
    def _():
        abuf[pad - hist:pad, :] = abuf[pad + tm - hist:pad + tm, :]

    a = _dot(hb, wg_ref[...])
    abuf[pad:pad + tm, :] = a
    cw = cw_ref[...]
    conv = cb_ref[...] + a * cw[CONV_W - 1:CONV_W, :]
    for i in range(CONV_W - 1):
        off = pad - (CONV_W - 1 - i) * step
        conv = conv + abuf[off:off + tm, :] * cw[i:i + 1, :]
    u = _dot(hb, wu_ref[...])
    act = conv * _sigmoid(conv) * u
    out = h + _dot(_bf(act), wd_ref[...])
    if final_norm:
        out = _rms(out, gfin_ref[...])
    out_ref[0] = out
    conv_out[0] = abuf[pad + tm - hist:pad + tm, :]


def _ffn(h, init, w, gfin, *, step, tm, final_norm):
    nb, tt, d = h.shape
    f = w['wg'].shape[1]
    hist = (CONV_W - 1) * step
    pad = max(SUBLANES, hist)
    row = pl.BlockSpec((1, tm, d), lambda b, t: (b, t, 0))
    cst = pl.BlockSpec((1, hist, f), lambda b, t: (b, 0, 0))
    weights = [w['gn'], w['wg'], w['wu'], w['cw'], w['cb'], w['wd'], gfin]
    return pl.pallas_call(
        functools.partial(_ffn_kernel, step=step, pad=pad, tm=tm, final_norm=final_norm),
        grid=(nb, tt // tm),
        in_specs=[row, cst] + [_const_spec(a.shape) for a in weights],
        out_specs=[row, cst],
        out_shape=[jax.ShapeDtypeStruct((nb, tt, d), F32), jax.ShapeDtypeStruct((nb, hist, f), F32)],
        scratch_shapes=[pltpu.VMEM((pad + tm, f), F32)],
        compiler_params=_params(("arbitrary", "arbitrary")),
        name="ffn_final" if final_norm else "ffn",
    )(h, init, *weights)


def _qkv_kernel(h_ref, gn_ref, w_ref, b_ref, q_out, k_out, v_out):
    d = h_ref.shape[-1]
    hb = _bf(_rms(h_ref[...], gn_ref[...]))
    qkv = _dot(hb, w_ref[...]) + b_ref[...]
    q_out[...] = qkv[:, :d]
    k_out[...] = qkv[:, d:2 * d]
    v_out[...] = qkv[:, 2 * d:]


def _qkv(h, w, *, tm):
    n, d = h.shape
    row = pl.BlockSpec((tm, d), lambda i: (i, 0))
    weights = [w['gn'], w['wqkv'], w['bqkv']]
    return pl.pallas_call(
        _qkv_kernel,
        grid=(n // tm,),
        in_specs=[row] + [_const_spec(a.shape) for a in weights],
        out_specs=[row] * 3,
        out_shape=[jax.ShapeDtypeStruct((n, d), F32)] * 3,
        compiler_params=_params(("arbitrary",)),
        name="sb_qkv",
    )(h, *weights)


def _qkv_kt_kernel(h_ref, gn_ref, w_ref, b_ref, wkt_ref, bk_ref, q_out, kt_out, v_out):
    d = h_ref.shape[-1]
    hb = _bf(_rms(h_ref[0], gn_ref[...]))
    qv = _dot(hb, w_ref[...]) + b_ref[...]
    q_out[0] = qv[:, :d]
    v_out[0] = qv[:, d:]
    kt_out[0] = _dot_nt(wkt_ref[...], hb) + bk_ref[...]


def _qkv_kt(h, w, *, tm):
    bsz, t, d = h.shape
    row = pl.BlockSpec((1, tm, d), lambda b, i: (b, i, 0))
    col = pl.BlockSpec((1, d, tm), lambda b, i: (b, 0, i))
    weights = [w['gn'], w['wqv'], w['bqv'], w['wkt'], w['bk_col']]
    return pl.pallas_call(
        _qkv_kt_kernel,
        grid=(bsz, t // tm),
        in_specs=[row] + [_const_spec(a.shape) for a in weights],
        out_specs=[row, col, row],
        out_shape=[jax.ShapeDtypeStruct((bsz, t, d), F32), jax.ShapeDtypeStruct((bsz, d, t), F32),
                   jax.ShapeDtypeStruct((bsz, t, d), F32)],
        compiler_params=_params(("arbitrary", "arbitrary")),
        name="sb_qkv_kt",
    )(h, *weights)


def _suffix_ones(n):
    r = lax.broadcasted_iota(jnp.int32, (n, 2 * n), 0)
    c = lax.broadcasted_iota(jnp.int32, (n, 2 * n), 1)
    return _indicator((r >= c) | (c >= n), BF16)


def _stick_stages(qm, kb, vb, valid, tri, run, keys_minor=False):
    nk = tri.shape[0]
    z = _dot(qm, kb) if keys_minor else _dot_nt(qm, kb)
    yield
    lb = _log_sigmoid(z)
    lk = lb - z
    if valid is not None:
        lk = jnp.where(valid, lk, 0.0)
    cs = _dot_sel(lk, tri)
    yield
    la = cs[:, :nk] - lk + run['carry']
    run['carry'] = run['carry'] + cs[:, nk:]
    att = jnp.exp(lb + la)
    if valid is not None:
        att = jnp.where(valid, att, 0.0)
    pv = _dot_nt(_bf(att), vb) if keys_minor else _dot(_bf(att), vb)
    yield
    run['acc'] = run['acc'] + pv


def _attn_kernel(q_ref, kt_ref, v_ref, o_ref, acc_ref, carry_ref, ktb, vgb, *, qb, kg, qu):
    nsub = kg // qb
    t = q_ref.shape[1]
    n_groups = t // kg
    n_qblocks = t // qb
    lane_head = lax.broadcasted_iota(jnp.int32, (1, LANES), 1) // HEAD_DIM
    tri = _suffix_ones(qb)

    def chain(qi, h, nblk, own_group):
        width = nblk * qb
        q = q_ref[0, pl.ds(pl.multiple_of(qi * qb, qb), qb), :] * (HEAD_DIM ** -0.5)
        z = _dot(_bf(jnp.where(lane_head == h, q, 0.0)), ktb[:, :width])
        yield
        lb = _log_sigmoid(z)
        lk = lb - z
        if own_group:
            row = lax.broadcasted_iota(jnp.int32, (qb, width), 0)
            col = lax.broadcasted_iota(jnp.int32, (qb, width), 1)
            valid = col < row + (nblk - 1) * qb
            lk = jnp.where(valid, lk, 0.0)
        hi, lo = _split2(lk)
        subs = [slice(s * qb, (s + 1) * qb) for s in range(nblk)]
        cs = [_dot(hi[:, sub], tri) + _dot(lo[:, sub], tri) for sub in subs]
        yield
        run = jnp.zeros((qb, qb), F32) if own_group else carry_ref[qi, h]
        las = [None] * nblk
        for s in reversed(range(nblk)):
            las[s] = cs[s][:, :qb] - lk[:, subs[s]] + run
            run = run + cs[s][:, qb:]
        carry_ref[qi, h] = run
        att = jnp.exp(lb + jnp.concatenate(las, axis=1))
        if own_group:
            att = jnp.where(valid, att, 0.0)
        pv = _dot(_bf(att), vgb[:width, :])
        yield
        acc_ref[qi, h] = pv if own_group else acc_ref[qi, h] + pv

    heads = range(HEADS_PER_TILE)

    def key_group(jj, carry):
        j = n_groups - 1 - jj
        start = pl.multiple_of(j * kg, kg)
        ktb[...] = _bf(kt_ref[0, :, pl.ds(start, kg)])
        vgb[...] = _bf(v_ref[0, pl.ds(start, kg), :])
        for r0 in range(0, nsub, qu):
            _interleave(chain(j * nsub + r, h, r + 1, True) for r in range(r0, r0 + qu) for h in heads)

        def later_queries(it, c):
            q0 = (j + 1) * nsub + it * qu
            _interleave(chain(q0 + u, h, nsub, False) for u in range(qu) for h in heads)
            return c

        lax.fori_loop(0, (n_groups - 1 - j) * (nsub // qu), later_queries, 0)
        return carry

    lax.fori_loop(0, n_groups, key_group, 0)

    def write_out(qi, carry):
        rows = pl.ds(pl.multiple_of(qi * qb, qb), qb)
        o_ref[0, rows, :] = jnp.where(lane_head == 0, acc_ref[qi, 0], acc_ref[qi, 1])
        return carry

    lax.fori_loop(0, n_qblocks, write_out, 0)


def _attn_prompt(q, kt, v, *, qb, kg):
    bsz, t, d = q.shape
    n_pairs = d // LANES
    qu = math.gcd(ATTN_QUERY_BLOCKS_PER_BODY, kg // qb)
    seq = pl.BlockSpec((1, t, LANES), lambda b, p: (b, 0, p))
    ks = pl.BlockSpec((1, LANES, t), lambda b, p: (b, p, 0))
    return pl.pallas_call(
        functools.partial(_attn_kernel, qb=qb, kg=kg, qu=qu),
        grid=(bsz, n_pairs),
        in_specs=[seq, ks, seq],
        out_specs=seq,
        out_shape=jax.ShapeDtypeStruct((bsz, t, d), F32),
        scratch_shapes=[pltpu.VMEM((t // qb, HEADS_PER_TILE, qb, LANES), F32),
                        pltpu.VMEM((t // qb, HEADS_PER_TILE, qb, qb), F32),
                        pltpu.VMEM((LANES, kg), BF16), pltpu.VMEM((kg, LANES), BF16)],
        compiler_params=_params(("arbitrary", "arbitrary")),
        name="sb_attn",
    )(q, kt, v)


def _paged_kernel(pt_ref, q_ref, kn_ref, vn_ref, *refs, tq, n_heads, page, group):
    kc_refs = refs[:group]
    vc_refs = refs[group:2 * group]
    o_ref, qexp, acc_ref, carry_ref, kpad, vpad = refs[2 * group:]
    p = pl.program_id(1)
    d = q_ref.shape[-1]
    nq = tq * n_heads
    tri = _suffix_ones(page)
    head_of_lane = lax.broadcasted_iota(jnp.int32, (n_heads, d), 1) // HEAD_DIM
    head_mask = head_of_lane == lax.broadcasted_iota(jnp.int32, (n_heads, d), 0)

    @pl.when(p == 0)
    def _():
        q = q_ref[0] * (HEAD_DIM ** -0.5)
        for t in range(tq):
            qexp[t * n_heads:(t + 1) * n_heads, :] = _bf(jnp.where(head_mask, q[t:t + 1, :], 0.0))
        kpad[...] = jnp.zeros(kpad.shape, F32)
        vpad[...] = jnp.zeros(vpad.shape, F32)
        kpad[0:tq, :] = kn_ref[0]
        vpad[0:tq, :] = vn_ref[0]
        row_t = lax.broadcasted_iota(jnp.int32, (nq, page), 0) // n_heads
        col = lax.broadcasted_iota(jnp.int32, (nq, page), 1)
        run = dict(carry=jnp.zeros((nq, page), F32), acc=jnp.zeros((nq, d), F32))
        _interleave([_stick_stages(qexp[...], _bf(kpad[...]), _bf(vpad[...]), col < row_t, tri, run)])
        acc_ref[...] = run['acc']
        carry_ref[...] = run['carry']

    run = dict(carry=carry_ref[...], acc=acc_ref[...])
    _interleave(_stick_stages(qexp[...], _bf(kc_ref[0]), _bf(vc_ref[0]), None, tri, run, keys_minor=True)
                for kc_ref, vc_ref in zip(kc_refs, vc_refs))
    acc = run['acc']
    acc_ref[...] = acc
    carry_ref[...] = run['carry']

    @pl.when(p == pl.num_programs(1) - 1)
    def _():
        rows = [jnp.sum(jnp.where(head_mask, acc[t * n_heads:(t + 1) * n_heads, :], 0.0), axis=0, keepdims=True)
                for t in range(tq)]
        o_ref[0] = jnp.concatenate(rows, axis=0)


def _attn_paged(q, k_new, v_new, cache_k, cache_v, page_table):
    bsz, tq, d = q.shape
    n_heads = d // HEAD_DIM
    n_pages = page_table.shape[1]
    page = cache_k.shape[2]
    nq = tq * n_heads
    group = math.gcd(PAGES_PER_STEP, n_pages)
    new = pl.BlockSpec((1, tq, d), lambda b, p, pt: (b, 0, 0))
    pgs = [pl.BlockSpec((1, d, page), lambda b, p, pt, g=g: (pt[b, n_pages - 1 - (p * group + g)], 0, 0))
           for g in range(group)]
    grid_spec = pltpu.PrefetchScalarGridSpec(
        num_scalar_prefetch=1,
        grid=(bsz, n_pages // group),
        in_specs=[new, new, new] + pgs + pgs,
        out_specs=new,
        scratch_shapes=[pltpu.VMEM((nq, d), BF16), pltpu.VMEM((nq, d), F32), pltpu.VMEM((nq, page), F32),
                        pltpu.VMEM((page, d), F32), pltpu.VMEM((page, d), F32)],
    )
    return pl.pallas_call(
        functools.partial(_paged_kernel, tq=tq, n_heads=n_heads, page=page, group=group),
        grid_spec=grid_spec,
        out_shape=jax.ShapeDtypeStruct((bsz, tq, d), F32),
        compiler_params=_params(("arbitrary", "arbitrary")),
        name="sb_paged",
    )(page_table, q, k_new, v_new, *([cache_k] * group), *([cache_v] * group))


def _proj_res_kernel(x_ref, res_ref, w_ref, out_ref):
    out_ref[...] = res_ref[...] + _dot(_bf(x_ref[...]), w_ref[...])


def _proj_res(x, res, w, *, tm):
    n, d = x.shape
    row = pl.BlockSpec((tm, d), lambda i: (i, 0))
    return pl.pallas_call(
        _proj_res_kernel,
        grid=(n // tm,),
        in_specs=[row, row, _const_spec(w.shape)],
        out_specs=row,
        out_shape=jax.ShapeDtypeStruct((n, d), F32),
        compiler_params=_params(("arbitrary",)),
        name="sb_out",
    )(x, res, w)


def _weights(norm_mix, norm_ffn, norm_final, tm_mu, tm_w_r, tm_w_k, tm_w_v, tm_w_o, tm_w0, tm_w_w1, tm_w_w2,
             tm_a0, tm_w_a1, tm_w_a2, tm_w_g1, tm_w_g2, tm_k_k, tm_k_a, tm_r_k, tm_gn_w, tm_gn_b,
             sb_w_qkv, sb_b_qkv, sb_w_o, ffn_w_gate, ffn_w_up, ffn_conv_w, ffn_conv_b, ffn_w_down):
    row = lambda a: a.reshape(1, -1)
    tm = dict(gmix=row(norm_mix[0]), mu=tm_mu[0], wr=_bf(tm_w_r[0]), wk=_bf(tm_w_k[0]), wv=_bf(tm_w_v[0]),
              w1=_bf(tm_w_w1[0]), w2=_bf(tm_w_w2[0]), w0=row(tm_w0[0]), a1=_bf(tm_w_a1[0]), a2=_bf(tm_w_a2[0]),
              a0=row(tm_a0[0]), g1=_bf(tm_w_g1[0]), g2=_bf(tm_w_g2[0]), k_k=row(tm_k_k[0]), k_a=row(tm_k_a[0]),
              gn_w=row(tm_gn_w[0]), gn_b=row(tm_gn_b[0]), r_k=row(tm_r_k[0]), wo=_bf(tm_w_o[0]))
    d = norm_final.shape[0]
    wqkv, bqkv = sb_w_qkv[0], sb_b_qkv[0]
    qv_cols = lambda a: jnp.concatenate([a[..., :d], a[..., 2 * d:]], axis=-1)
    sb = dict(gn=row(norm_mix[1]), wqkv=_bf(wqkv), bqkv=row(bqkv), wo=_bf(sb_w_o[0]),
              wqv=_bf(qv_cols(wqkv)), bqv=row(qv_cols(bqkv)), wkt=_bf(wqkv[:, d:2 * d].T),
              bk_col=bqkv[d:2 * d].reshape(d, 1))
    ffn = [dict(gn=row(norm_ffn[i]), wg=_bf(ffn_w_gate[i]), wu=_bf(ffn_w_up[i]), cw=ffn_conv_w[i],
                cb=row(ffn_conv_b[i]), wd=_bf(ffn_w_down[i])) for i in range(2)]
    return tm, sb, ffn, row(norm_final)


def _trunk_prompt(x, w_tm, w_sb, w_ffn, g_final, *, tm, chunk, qb, kg):
    bsz, t, d = x.shape
    n = bsz * t
    f = w_ffn[0]['wg'].shape[1]
    flat = lambda a: a.reshape(n, d)
    r, lw, kh, v, kk, kka, g, shift = _tm_pre(x, jnp.zeros((bsz, 1, d), F32), w_tm, step=1, tm=tm)
    s0 = jnp.zeros((bsz, d // HEAD_DIM, HEAD_DIM, HEAD_DIM), F32)
    y, wkv = _wkv(r, lw, kh, v, kk, kka, s0, c=chunk)
    tm_mid = min(DENSE_TILE, t)
    tm_out = min(OUT_PROJ_TILE, t)
    h = _tm_post(flat(y), flat(r), flat(kh), flat(v), flat(g), flat(x), w_tm, tm=tm_mid)
    conv0 = jnp.zeros((bsz, CONV_W - 1, f), F32)
    h, conv_a = _ffn(h.reshape(bsz, t, d), conv0, w_ffn[0], g_final, step=1, tm=tm, final_norm=False)
    q, kt, v2 = _qkv_kt(h, w_sb, tm=tm_mid)
    o = _attn_prompt(q, kt, v2, qb=qb, kg=kg)
    h = _proj_res(flat(o), flat(h), w_sb['wo'], tm=tm_out)
    out, conv_b = _ffn(h.reshape(bsz, t, d), conv0, w_ffn[1], g_final, step=1, tm=tm, final_norm=True)
    n_heads = d // HEAD_DIM
    k_new = jnp.transpose(kt.reshape(bsz, n_heads, HEAD_DIM, t), (0, 3, 1, 2))
    return (out, shift.reshape(1, bsz, d), wkv[None], k_new[None], v2.reshape(bsz, t, n_heads, HEAD_DIM)[None],
            jnp.stack([conv_a, conv_b]))


def _trunk_sample(x, state_shift, state_wkv, state_conv, cache_k, cache_v, page_table, w_tm, w_sb, w_ffn, g_final,
                  *, chunk):
    bsz, t, d = x.shape
    n = bsz * t
    f = w_ffn[0]['wg'].shape[1]
    to_tm = lambda a: jnp.swapaxes(a, 0, 1).reshape(1, n, a.shape[-1])
    to_bm = lambda a: jnp.swapaxes(a.reshape(t, bsz, a.shape[-1]), 0, 1)
    xs = to_tm(x)
    r, lw, kh, v, kk, kka, g, shift = _tm_pre(xs, state_shift[0][None], w_tm, step=bsz, tm=n)
    pad_t = lambda a: jnp.pad(to_bm(a), ((0, 0), (0, chunk - t), (0, 0)))
    y, wkv = _wkv(pad_t(r), pad_t(lw), pad_t(kh), pad_t(v), pad_t(kk), pad_t(kka), state_wkv[0], c=chunk)
    y = to_tm(y[:, :t])
    flat = lambda a: a.reshape(n, d)
    h = _tm_post(flat(y), flat(r), flat(kh), flat(v), flat(g), flat(xs), w_tm, tm=n)
    conv_tm = lambda a: jnp.swapaxes(a, 0, 1).reshape(1, (CONV_W - 1) * bsz, f)
    conv_bm = lambda a: jnp.swapaxes(a.reshape(CONV_W - 1, bsz, f), 0, 1)
    h, conv_a = _ffn(h.reshape(1, n, d), conv_tm(state_conv[0]), w_ffn[0], g_final, step=bsz, tm=n,
                     final_norm=False)
    q, k, v2 = _qkv(flat(h), w_sb, tm=n)
    q, k, v2 = to_bm(q), to_bm(k), to_bm(v2)
    n_phys, page = cache_k.shape[1], cache_k.shape[2]
    keys_minor = lambda c: jnp.transpose(c[0], (0, 2, 3, 1)).reshape(n_phys, d, page)
    o = _attn_paged(q, k, v2, keys_minor(cache_k), keys_minor(cache_v), page_table)
    h = _proj_res(flat(to_tm(o)), flat(h), w_sb['wo'], tm=n)
    out, conv_b = _ffn(h.reshape(1, n, d), conv_tm(state_conv[1]), w_ffn[1], g_final, step=bsz, tm=n,
                       final_norm=True)
    hd = (bsz, t, d // HEAD_DIM, HEAD_DIM)
    return (to_bm(out[0]), shift.reshape(1, bsz, d), wkv[None], k.reshape(hd)[None], v2.reshape(hd)[None],
            jnp.stack([conv_bm(conv_a[0]), conv_bm(conv_b[0])]))


PROMPT_TILE = 256
DENSE_TILE = 512
OUT_PROJ_TILE = 1024
WKV_TILES_PER_STEP = 8
WKV_CHUNK = 64
WKV_INVERSE_BASE = 8
SAMPLE_WKV_CHUNK = 8
Q_BLOCK = 128
KEY_GROUP = 512
ATTN_QUERY_BLOCKS_PER_BODY = 4
PAGES_PER_STEP = 8


def kernel(x_prompt, x_sample, state_shift, state_wkv, cache_k, cache_v, state_conv, page_table, norm_mix, norm_ffn, norm_final, tm_mu, tm_w_r, tm_w_k, tm_w_v, tm_w_o, tm_w0, tm_w_w1, tm_w_w2, tm_a0, tm_w_a1, tm_w_a2, tm_w_g1, tm_w_g2, tm_k_k, tm_k_a, tm_r_k, tm_gn_w, tm_gn_b, sb_w_qkv, sb_b_qkv, sb_w_o, ffn_w_gate, ffn_w_up, ffn_conv_w, ffn_conv_b, ffn_w_down):
    w_tm, w_sb, w_ffn, g_final = _weights(
        norm_mix, norm_ffn, norm_final, tm_mu, tm_w_r, tm_w_k, tm_w_v, tm_w_o, tm_w0, tm_w_w1, tm_w_w2,
        tm_a0, tm_w_a1, tm_w_a2, tm_w_g1, tm_w_g2, tm_k_k, tm_k_a, tm_r_k, tm_gn_w, tm_gn_b,
        sb_w_qkv, sb_b_qkv, sb_w_o, ffn_w_gate, ffn_w_up, ffn_conv_w, ffn_conv_b, ffn_w_down)
    tile = min(PROMPT_TILE, x_prompt.shape[1])
    y_p, shift_p, wkv_p, k_p, v_p, conv_p = _trunk_prompt(
        x_prompt, w_tm, w_sb, w_ffn, g_final, tm=tile, chunk=min(WKV_CHUNK, x_prompt.shape[1]),
        qb=min(Q_BLOCK, x_prompt.shape[1]), kg=min(KEY_GROUP, x_prompt.shape[1]))
    y_s, shift_s, wkv_s, k_s, v_s, conv_s = _trunk_sample(
        x_sample, state_shift, state_wkv, state_conv, cache_k, cache_v, page_table, w_tm, w_sb, w_ffn, g_final,
        chunk=SAMPLE_WKV_CHUNK)
    return (y_p, y_s, shift_p, wkv_p, k_p, v_p, conv_p, shift_s, wkv_s, k_s, v_s, conv_s)
```

```python
import functools
import math

import jax
import jax.numpy as jnp
from jax import lax
from jax.experimental import pallas as pl
from jax.experimental.pallas import tpu as pltpu

HEAD_DIM = 64
LANES = 128
SUBLANES = 8
HEADS_PER_TILE = LANES // HEAD_DIM
CONV_W = 3
RMS_EPS = 1e-6
GN_EPS = 64e-5
KK_EPS = 1e-12
N_MIX = 6
VMEM_LIMIT = 56 * 1024 * 1024

F32 = jnp.float32
BF16 = jnp.bfloat16


def _bf(x):
    return x.astype(BF16)


def _indicator(mask, dtype=F32):
    return jnp.where(mask, 1.0, 0.0).astype(dtype)


def _dot(a, b):
    return jnp.dot(a, b, preferred_element_type=F32)


def _dot_nt(a, b):
    return lax.dot_general(a, b, (((1,), (1,)), ((), ())), preferred_element_type=F32)


def _dot_tn(a, b):
    return lax.dot_general(a, b, (((0,), (0,)), ((), ())), preferred_element_type=F32)


def _split2(x):
    hi = x.astype(BF16)
    lo = (x - hi.astype(F32)).astype(BF16)
    return hi, lo


def _split3(x):
    hi = x.astype(BF16)
    r1 = x - hi.astype(F32)
    mid = r1.astype(BF16)
    lo = (r1 - mid.astype(F32)).astype(BF16)
    return hi, mid, lo


def _dot_sel(x, m):
    hi, lo = _split2(x)
    return _dot(hi, m) + _dot(lo, m)


def _dot_hp(a, b, dot=_dot):
    ah, al = _split2(a)
    bh, bl = _split2(b)
    return dot(ah, bh) + (dot(ah, bl) + dot(al, bh))


def _rms(x, g):
    return x * lax.rsqrt(jnp.mean(x * x, axis=-1, keepdims=True) + RMS_EPS) * g


def _softplus(x):
    return jnp.maximum(x, 0.0) + jnp.log(1.0 + jnp.exp(-jnp.abs(x)))


def _sigmoid(x):
    return 1.0 / (1.0 + jnp.exp(-x))


def _head_ones():
    r = lax.broadcasted_iota(jnp.int32, (LANES, LANES), 0) // HEAD_DIM
    c = lax.broadcasted_iota(jnp.int32, (LANES, LANES), 1) // HEAD_DIM
    return _indicator(r == c, BF16)


def _head_sum(x, ones_bd):
    d = x.shape[-1]
    cols = [_dot_sel(x[:, g * LANES:(g + 1) * LANES], ones_bd) for g in range(d // LANES)]
    return jnp.concatenate(cols, axis=1)


def _interleave(stage_generators):
    live = list(stage_generators)
    done = object()
    while live:
        live = [g for g in live if next(g, done) is not done]


def _const_spec(shape):
    nd = len(shape)
    return pl.BlockSpec(shape, lambda *_: (0,) * nd, pipeline_mode=pl.Buffered(1))


def _params(sem):
    return pltpu.CompilerParams(dimension_semantics=sem, vmem_limit_bytes=VMEM_LIMIT)


def _tm_pre_kernel(x_ref, init_ref, gmix_ref, mu_ref, wr_ref, wk_ref, wv_ref, w1_ref, w2_ref, w0_ref,
                   a1_ref, a2_ref, a0_ref, g1_ref, g2_ref, kk_ref, ka_ref,
                   r_out, lw_out, kh_out, v_out, kk_out, kka_out, g_out, shift_out, hbuf, *, step, pad, tm):
    t = pl.program_id(1)
    hn = _rms(x_ref[0], gmix_ref[...])

    @pl.when(t == 0)
    def _():
        hbuf[pad - step:pad, :] = init_ref[0]

    @pl.when(t > 0)
    def _():
        hbuf[pad - step:pad, :] = hbuf[pad + tm - step:pad + tm, :]

    hbuf[pad:pad + tm, :] = hn
    xx = hbuf[pad - step:pad - step + tm, :] - hn
    mu = mu_ref[...]
    xr, xw, xk, xv, xa, xg = [_bf(hn + xx * mu[i:i + 1, :]) for i in range(N_MIX)]
    r = _dot(xr, wr_ref[...])
    k = _dot(xk, wk_ref[...])
    v = _dot(xv, wv_ref[...])
    wl = w0_ref[...] + _dot(_bf(jnp.tanh(_dot(xw, w1_ref[...]))), w2_ref[...])
    w_log = -_softplus(-wl) - 0.5
    a = _sigmoid(a0_ref[...] + _dot(_bf(_dot(xa, a1_ref[...])), a2_ref[...]))
    g = _dot(_bf(_sigmoid(_dot(xg, g1_ref[...]))), g2_ref[...])
    kk = k * kk_ref[...]
    nrm = jnp.sqrt(_head_sum(kk * kk, _head_ones()))
    kk = kk / jnp.maximum(nrm, KK_EPS)
    r_out[0] = r
    lw_out[0] = -jnp.exp(w_log)
    kh_out[0] = k * (1.0 + (a - 1.0) * ka_ref[...])
    v_out[0] = v
    kk_out[0] = kk
    kka_out[0] = kk * a
    g_out[0] = g
    shift_out[0] = hn[tm - step:, :]


def _tm_pre(x, init, w, *, step, tm):
    nb, tt, d = x.shape
    pad = max(SUBLANES, step)
    nt = tt // tm
    row = pl.BlockSpec((1, tm, d), lambda b, t: (b, t, 0))
    weights = [w['gmix'], w['mu'], w['wr'], w['wk'], w['wv'], w['w1'], w['w2'], w['w0'],
               w['a1'], w['a2'], w['a0'], w['g1'], w['g2'], w['k_k'], w['k_a']]
    outs = pl.pallas_call(
        functools.partial(_tm_pre_kernel, step=step, pad=pad, tm=tm),
        grid=(nb, nt),
        in_specs=[row, pl.BlockSpec((1, step, d), lambda b, t: (b, 0, 0))] + [_const_spec(a.shape) for a in weights],
        out_specs=[row] * 7 + [pl.BlockSpec((1, step, d), lambda b, t: (b, 0, 0))],
        out_shape=[jax.ShapeDtypeStruct((nb, tt, d), F32)] * 7 + [jax.ShapeDtypeStruct((nb, step, d), F32)],
        scratch_shapes=[pltpu.VMEM((pad + tm, d), F32)],
        compiler_params=_params(("arbitrary", "arbitrary")),
        name="tm_pre",
    )(x, init, *weights)
    return outs


def _wkv_kernel(r_ref, lw_ref, k_ref, v_ref, kk_ref, kka_ref, s0_ref, y_ref, sT_ref, s_scr, *, c, tiles):
    ci = pl.program_id(2)
    c2 = 2 * c
    lane_head = lax.broadcasted_iota(jnp.int32, (1, LANES), 1) // HEAD_DIM
    m0 = _indicator(lane_head == 0)
    m1 = 1.0 - m0
    hr = lax.broadcasted_iota(jnp.int32, (LANES, LANES), 0) // HEAD_DIM
    hc = lax.broadcasted_iota(jnp.int32, (LANES, LANES), 1) // HEAD_DIM
    same_head = hr == hc
    trow = lax.broadcasted_iota(jnp.int32, (c, c), 0)
    tcol = lax.broadcasted_iota(jnp.int32, (c, c), 1)
    lower = _indicator(tcol <= trow, BF16)
    rr = lax.broadcasted_iota(jnp.int32, (c2, c2), 0)
    cc = lax.broadcasted_iota(jnp.int32, (c2, c2), 1)
    ti = rr % c
    tj = cc % c
    top = rr < c
    left = cc < c
    keep0 = tj < ti + jnp.where(top, 0, 1)
    keep1 = tj < ti + jnp.where(top, 1, 0)
    left_half = lax.broadcasted_iota(jnp.int32, (c, c2), 1) < c
    wide = c2 % LANES == 0
    eye = _indicator(rr == cc)
    base = min(WKV_INVERSE_BASE, c)
    n_base = int(math.log2(base))

    @pl.when(ci == 0)
    def _():
        s_scr[...] = s0_ref[0]

    def tile_stages(tile):
        lanes = slice(tile * LANES, (tile + 1) * LANES)
        r = r_ref[0, :, lanes]
        lw = lw_ref[0, :, lanes]
        k = k_ref[0, :, lanes]
        v = v_ref[0, :, lanes]
        a = -kk_ref[0, :, lanes]
        b = kka_ref[0, :, lanes]
        s0 = s_scr[tile]

        h1, h2, h3 = _split3(lw)
        cum = _dot(lower, h1) + (_dot(lower, h2) + _dot(lower, h3))
        yield
        cum_c = cum[c - 1:c, :]
        a_t = a * jnp.exp(cum - lw)
        r_t = r * jnp.exp(cum)
        einv = jnp.exp(-cum)
        b_t = b * einv
        k_t = k * einv
        eend = jnp.exp(cum_c - cum)
        b_e = b * eend
        k_e = k * eend

        sc0 = _dot_hp(jnp.concatenate([a_t * m0, r_t * m0], axis=0), jnp.concatenate([b_t, k_t], axis=0), _dot_nt)
        sc1 = _dot_hp(jnp.concatenate([r_t * m1, a_t * m1], axis=0), jnp.concatenate([k_t, b_t], axis=0), _dot_nt)
        ar_s = _dot_hp(jnp.concatenate([a_t, r_t], axis=0), s0, _dot_nt)
        yield
        sc0 = jnp.where(keep0, sc0, 0.0)
        sc1 = jnp.where(keep1, sc1, 0.0)
        v0 = v * m0
        v1 = v * m1
        ak = jnp.where(left_half, sc1[c:], sc0[:c])
        rhs = ar_s[:c] + _dot_hp(ak, jnp.concatenate([v1, v0], axis=0))
        yield

        n_bd = jnp.where(top & left, sc0, 0.0) + jnp.where((~top) & (~left), sc1, 0.0)
        p = jnp.where(ti // base == tj // base, n_bd, 0.0)
        t_inv = eye + p
        if n_base > 1:
            p = _dot_hp(p, p)
            yield
        for i in range(1, n_base):
            if i == n_base - 1:
                t_inv = t_inv + _dot_hp(p, t_inv)
            elif wide:
                pt = _dot_hp(p, jnp.concatenate([p, t_inv], axis=1))
                p, t_inv = pt[:, :c2], t_inv + pt[:, c2:]
            else:
                t_inv = t_inv + _dot_hp(p, t_inv)
                yield
                p = _dot_hp(p, p)
            yield
        m = base
        while m < c:
            n_off = jnp.where((ti // (2 * m) == tj // (2 * m)) & (ti // m != tj // m), n_bd, 0.0)
            nt = _dot_hp(n_off, t_inv)
            yield
            t_inv = t_inv + _dot_hp(t_inv, nt)
            yield
            m *= 2
        x = _dot_hp(t_inv, jnp.concatenate([rhs * m0, rhs * m1], axis=0))
        yield
        u0 = x[:c]
        u1 = x[c:]

        if wide:
            y_uv = _dot_hp(jnp.concatenate([sc0[c:], sc1[:c]], axis=1), jnp.concatenate([u0, v0, v1, u1], axis=0))
        else:
            y_uv = (_dot_hp(sc0[c:], jnp.concatenate([u0, v0], axis=0))
                    + _dot_hp(sc1[:c], jnp.concatenate([v1, u1], axis=0)))
        upd = _dot_hp(jnp.concatenate([u0 + u1, v], axis=0), jnp.concatenate([b_e, k_e], axis=0), _dot_tn)
        yield
        y_ref[0, :, lanes] = ar_s[c:] + y_uv
        s_scr[tile] = s0 * jnp.exp(cum_c) + jnp.where(same_head, upd, 0.0)

    _interleave(tile_stages(tile) for tile in range(tiles))

    @pl.when(ci == pl.num_programs(2) - 1)
    def _():
        sT_ref[0] = s_scr[...]


def _wkv(r, lw, kh, v, kk, kka, s0, *, c):
    bsz, t, d = r.shape
    n_pairs = d // LANES
    assert HEADS_PER_TILE == 2
    s0_pairs = s0.reshape(bsz, n_pairs, HEADS_PER_TILE, HEAD_DIM, HEAD_DIM)
    zero = jnp.zeros((bsz, n_pairs, HEAD_DIM, HEAD_DIM), F32)
    s0_bd = jnp.concatenate([jnp.concatenate([s0_pairs[:, :, 0], zero], axis=-1),
                             jnp.concatenate([zero, s0_pairs[:, :, 1]], axis=-1)], axis=-2)
    tiles = WKV_TILES_PER_STEP
    seq = pl.BlockSpec((1, c, tiles * LANES), lambda b, p, i: (b, i, p))
    st = pl.BlockSpec((1, tiles, LANES, LANES), lambda b, p, i: (b, p, 0, 0))
    y, s_bd = pl.pallas_call(
        functools.partial(_wkv_kernel, c=c, tiles=tiles),
        grid=(bsz, n_pairs // tiles, t // c),
        in_specs=[seq] * 6 + [st],
        out_specs=[seq, st],
        out_shape=[jax.ShapeDtypeStruct((bsz, t, d), F32), jax.ShapeDtypeStruct(s0_bd.shape, F32)],
        scratch_shapes=[pltpu.VMEM((tiles, LANES, LANES), F32)],
        compiler_params=_params(("arbitrary", "arbitrary", "arbitrary")),
        name="wkv",
    )(r, lw, kh, v, kk, kka, s0_bd)
    s_fin = jnp.stack([s_bd[:, :, :HEAD_DIM, :HEAD_DIM], s_bd[:, :, HEAD_DIM:, HEAD_DIM:]], axis=2)
    return y, s_fin.reshape(s0.shape)


def _tm_post_kernel(y_ref, r_ref, kh_ref, v_ref, g_ref, x_ref, gnw_ref, gnb_ref, rk_ref, wo_ref, out_ref):
    ones_bd = _head_ones()
    y = y_ref[...]
    mean = _head_sum(y, ones_bd) * (1.0 / HEAD_DIM)
    yc = y - mean
    var = _head_sum(yc * yc, ones_bd) * (1.0 / HEAD_DIM)
    yn = yc * lax.rsqrt(var + GN_EPS) * gnw_ref[...] + gnb_ref[...]
    bonus = _head_sum(r_ref[...] * kh_ref[...] * rk_ref[...], ones_bd) * v_ref[...]
    out_ref[...] = x_ref[...] + _dot(_bf((yn + bonus) * g_ref[...]), wo_ref[...])


def _tm_post(y, r, kh, v, g, x, w, *, tm):
    n, d = x.shape
    row = pl.BlockSpec((tm, d), lambda i: (i, 0))
    weights = [w['gn_w'], w['gn_b'], w['r_k'], w['wo']]
    return pl.pallas_call(
        _tm_post_kernel,
        grid=(n // tm,),
        in_specs=[row] * 6 + [_const_spec(a.shape) for a in weights],
        out_specs=row,
        out_shape=jax.ShapeDtypeStruct((n, d), F32),
        compiler_params=_params(("arbitrary",)),
        name="tm_post",
    )(y, r, kh, v, g, x, *weights)


def _ffn_kernel(h_ref, init_ref, gn_ref, wg_ref, wu_ref, cw_ref, cb_ref, wd_ref, gfin_ref,
                out_ref, conv_out, abuf, *, step, pad, tm, final_norm):
    t = pl.program_id(1)
    hist = (CONV_W - 1) * step
    h = h_ref[0]
    hb = _bf(_rms(h, gn_ref[...]))

    @pl.when(t == 0)
    def _():
        abuf[pad - hist:pad, :] = init_ref[0]

    @pl.when(t > 0)
    def _():
        abuf[pad - hist:pad, :] = abuf[pad + tm - hist:pad + tm, :]

    a = _dot(hb, wg_ref[...])
    abuf[pad:pad + tm, :] = a
    cw = cw_ref[...]
    conv = cb_ref[...] + a * cw[CONV_W - 1:CONV_W, :]
    for i in range(CONV_W - 1):
        off = pad - (CONV_W - 1 - i) * step
        conv = conv + abuf[off:off + tm, :] * cw[i:i + 1, :]
    u = _dot(hb, wu_ref[...])
    act = conv * _sigmoid(conv) * u
    out = h + _dot(_bf(act), wd_ref[...])
    if final_norm:
        out = _rms(out, gfin_ref[...])
    out_ref[0] = out
    conv_out[0] = abuf[pad + tm - hist:pad + tm, :]


def _ffn(h, init, w, gfin, *, step, tm, final_norm):
    nb, tt, d = h.shape
    f = w['wg'].shape[1]
    hist = (CONV_W - 1) * step
    pad = max(SUBLANES, hist)
    row = pl.BlockSpec((1, tm, d), lambda b, t: (b, t, 0))
    cst = pl.BlockSpec((1, hist, f), lambda b, t: (b, 0, 0))
    weights = [w['gn'], w['wg'], w['wu'], w['cw'], w['cb'], w['wd'], gfin]
    return pl.pallas_call(
        functools.partial(_ffn_kernel, step=step, pad=pad, tm=tm, final_norm=final_norm),
        grid=(nb, tt // tm),
        in_specs=[row, cst] + [_const_spec(a.shape) for a in weights],
        out_specs=[row, cst],
        out_shape=[jax.ShapeDtypeStruct((nb, tt, d), F32), jax.ShapeDtypeStruct((nb, hist, f), F32)],
        scratch_shapes=[pltpu.VMEM((pad + tm, f), F32)],
        compiler_params=_params(("arbitrary", "arbitrary")),
        name="ffn_final" if final_norm else "ffn",
    )(h, init, *weights)


def _qkv_kernel(h_ref, gn_ref, w_ref, b_ref, q_out, k_out, v_out):
    d = h_ref.shape[-1]
    hb = _bf(_rms(h_ref[...], gn_ref[...]))
    qkv = _dot(hb, w_ref[...]) + b_ref[...]
    q_out[...] = qkv[:, :d]
    k_out[...] = qkv[:, d:2 * d]
    v_out[...] = qkv[:, 2 * d:]


def _qkv(h, w, *, tm):
    n, d = h.shape
    row = pl.BlockSpec((tm, d), lambda i: (i, 0))
    weights = [w['gn'], w['wqkv'], w['bqkv']]
    return pl.pallas_call(
        _qkv_kernel,
        grid=(n // tm,),
        in_specs=[row] + [_const_spec(a.shape) for a in weights],
        out_specs=[row] * 3,
        out_shape=[jax.ShapeDtypeStruct((n, d), F32)] * 3,
        compiler_params=_params(("arbitrary",)),
        name="sb_qkv",
    )(h, *weights)


def _qkv_kt_kernel(h_ref, gn_ref, w_ref, b_ref, wkt_ref, bk_ref, q_out, kt_out, v_out):
    d = h_ref.shape[-1]
    hb = _bf(_rms(h_ref[0], gn_ref[...]))
    qv = _dot(hb, w_ref[...]) + b_ref[...]
    q_out[0] = qv[:, :d]
    v_out[0] = qv[:, d:]
    kt_out[0] = _dot_nt(wkt_ref[...], hb) + bk_ref[...]


def _qkv_kt(h, w, *, tm):
    bsz, t, d = h.shape
    row = pl.BlockSpec((1, tm, d), lambda b, i: (b, i, 0))
    col = pl.BlockSpec((1, d, tm), lambda b, i: (b, 0, i))
    weights = [w['gn'], w['wqv'], w['bqv'], w['wkt'], w['bk_col']]
    return pl.pallas_call(
        _qkv_kt_kernel,
        grid=(bsz, t // tm),
        in_specs=[row] + [_const_spec(a.shape) for a in weights],
        out_specs=[row, col, row],
        out_shape=[jax.ShapeDtypeStruct((bsz, t, d), F32), jax.ShapeDtypeStruct((bsz, d, t), F32),
                   jax.ShapeDtypeStruct((bsz, t, d), F32)],
        compiler_params=_params(("arbitrary", "arbitrary")),
        name="sb_qkv_kt",
    )(h, *weights)


def _suffix_ones(n):
    r = lax.broadcasted_iota(jnp.int32, (n, 2 * n), 0)
    c = lax.broadcasted_iota(jnp.int32, (n, 2 * n), 1)
    return _indicator((r >= c) | (c >= n), BF16)


def _stick_stages(qm, kb, vb, valid, tri, run, keys_minor=False):
    nk = tri.shape[0]
    z = _dot(qm, kb) if keys_minor else _dot_nt(qm, kb)
    yield
    drop = _softplus(z)
    if valid is not None:
        drop = jnp.where(valid, drop, 0.0)
    cs = _dot_sel(drop, tri)
    yield
    att = jnp.exp(z - (cs[:, :nk] + run['carry']))
    run['carry'] = run['carry'] + cs[:, nk:]
    if valid is not None:
        att = jnp.where(valid, att, 0.0)
    pv = _dot_nt(_bf(att), vb) if keys_minor else _dot(_bf(att), vb)
    yield
    run['acc'] = run['acc'] + pv


def _attn_kernel(q_ref, kt_ref, v_ref, o_ref, acc_ref, carry_ref, ktb, vgb, *, qb, kg, qu):
    nsub = kg // qb
    t = q_ref.shape[1]
    n_groups = t // kg
    n_qblocks = t // qb
    lane_head = lax.broadcasted_iota(jnp.int32, (1, LANES), 1) // HEAD_DIM
    tri = _suffix_ones(qb)

    def chain(qi, h, nblk, own_group):
        width = nblk * qb
        q = q_ref[0, pl.ds(pl.multiple_of(qi * qb, qb), qb), :] * (HEAD_DIM ** -0.5)
        z = _dot(_bf(jnp.where(lane_head == h, q, 0.0)), ktb[:, :width])
        yield
        drop = _softplus(z)
        if own_group:
            row = lax.broadcasted_iota(jnp.int32, (qb, width), 0)
            col = lax.broadcasted_iota(jnp.int32, (qb, width), 1)
            valid = col < row + (nblk - 1) * qb
            drop = jnp.where(valid, drop, 0.0)
        hi, lo = _split2(drop)
        subs = [slice(s * qb, (s + 1) * qb) for s in range(nblk)]
        cs = [_dot(hi[:, sub], tri) + _dot(lo[:, sub], tri) for sub in subs]
        yield
        run = jnp.zeros((qb, qb), F32) if own_group else carry_ref[qi, h]
        later = [None] * nblk
        for s in reversed(range(nblk)):
            later[s] = cs[s][:, :qb] + run
            run = run + cs[s][:, qb:]
        carry_ref[qi, h] = run
        att = jnp.exp(z - jnp.concatenate(later, axis=1))
        if own_group:
            att = jnp.where(valid, att, 0.0)
        pv = _dot(_bf(att), vgb[:width, :])
        yield
        acc_ref[qi, h] = pv if own_group else acc_ref[qi, h] + pv

    heads = range(HEADS_PER_TILE)

    def key_group(jj, carry):
        j = n_groups - 1 - jj
        start = pl.multiple_of(j * kg, kg)
        ktb[...] = _bf(kt_ref[0, :, pl.ds(start, kg)])
        vgb[...] = _bf(v_ref[0, pl.ds(start, kg), :])
        for r0 in range(0, nsub, qu):
            _interleave(chain(j * nsub + r, h, r + 1, True) for r in range(r0, r0 + qu) for h in heads)

        def later_queries(it, c):
            q0 = (j + 1) * nsub + it * qu
            _interleave(chain(q0 + u, h, nsub, False) for u in range(qu) for h in heads)
            return c

        lax.fori_loop(0, (n_groups - 1 - j) * (nsub // qu), later_queries, 0)
        return carry

    lax.fori_loop(0, n_groups, key_group, 0)

    def write_out(qi, carry):
        rows = pl.ds(pl.multiple_of(qi * qb, qb), qb)
        o_ref[0, rows, :] = jnp.where(lane_head == 0, acc_ref[qi, 0], acc_ref[qi, 1])
        return carry

    lax.fori_loop(0, n_qblocks, write_out, 0)


def _attn_prompt(q, kt, v, *, qb, kg):
    bsz, t, d = q.shape
    n_pairs = d // LANES
    qu = math.gcd(ATTN_QUERY_BLOCKS_PER_BODY, kg // qb)
    seq = pl.BlockSpec((1, t, LANES), lambda b, p: (b, 0, p))
    ks = pl.BlockSpec((1, LANES, t), lambda b, p: (b, p, 0))
    return pl.pallas_call(
        functools.partial(_attn_kernel, qb=qb, kg=kg, qu=qu),
        grid=(bsz, n_pairs),
        in_specs=[seq, ks, seq],
        out_specs=seq,
        out_shape=jax.ShapeDtypeStruct((bsz, t, d), F32),
        scratch_shapes=[pltpu.VMEM((t // qb, HEADS_PER_TILE, qb, LANES), F32),
                        pltpu.VMEM((t // qb, HEADS_PER_TILE, qb, qb), F32),
                        pltpu.VMEM((LANES, kg), BF16), pltpu.VMEM((kg, LANES), BF16)],
        compiler_params=_params(("arbitrary", "arbitrary")),
        name="sb_attn",
    )(q, kt, v)


def _paged_kernel(pt_ref, q_ref, kn_ref, vn_ref, *refs, tq, n_heads, page, group):
    kc_refs = refs[:group]
    vc_refs = refs[group:2 * group]
    o_ref, qexp, acc_ref, carry_ref, kpad, vpad = refs[2 * group:]
    p = pl.program_id(1)
    d = q_ref.shape[-1]
    nq = tq * n_heads
    tri = _suffix_ones(page)
    head_of_lane = lax.broadcasted_iota(jnp.int32, (n_heads, d), 1) // HEAD_DIM
    head_mask = head_of_lane == lax.broadcasted_iota(jnp.int32, (n_heads, d), 0)

    @pl.when(p == 0)
    def _():
        q = q_ref[0] * (HEAD_DIM ** -0.5)
        for t in range(tq):
            qexp[t * n_heads:(t + 1) * n_heads, :] = _bf(jnp.where(head_mask, q[t:t + 1, :], 0.0))
        kpad[...] = jnp.zeros(kpad.shape, F32)
        vpad[...] = jnp.zeros(vpad.shape, F32)
        kpad[0:tq, :] = kn_ref[0]
        vpad[0:tq, :] = vn_ref[0]
        row_t = lax.broadcasted_iota(jnp.int32, (nq, page), 0) // n_heads
        col = lax.broadcasted_iota(jnp.int32, (nq, page), 1)
        run = dict(carry=jnp.zeros((nq, page), F32), acc=jnp.zeros((nq, d), F32))
        _interleave([_stick_stages(qexp[...], _bf(kpad[...]), _bf(vpad[...]), col < row_t, tri, run)])
        acc_ref[...] = run['acc']
        carry_ref[...] = run['carry']

    run = dict(carry=carry_ref[...], acc=acc_ref[...])
    _interleave(_stick_stages(qexp[...], _bf(kc_ref[0]), _bf(vc_ref[0]), None, tri, run, keys_minor=True)
                for kc_ref, vc_ref in zip(kc_refs, vc_refs))
    acc = run['acc']
    acc_ref[...] = acc
    carry_ref[...] = run['carry']

    @pl.when(p == pl.num_programs(1) - 1)
    def _():
        rows = [jnp.sum(jnp.where(head_mask, acc[t * n_heads:(t + 1) * n_heads, :], 0.0), axis=0, keepdims=True)
                for t in range(tq)]
        o_ref[0] = jnp.concatenate(rows, axis=0)


def _attn_paged(q, k_new, v_new, cache_k, cache_v, page_table):
    bsz, tq, d = q.shape
    n_heads = d // HEAD_DIM
    n_pages = page_table.shape[1]
    page = cache_k.shape[2]
    nq = tq * n_heads
    group = math.gcd(PAGES_PER_STEP, n_pages)
    new = pl.BlockSpec((1, tq, d), lambda b, p, pt: (b, 0, 0))
    pgs = [pl.BlockSpec((1, d, page), lambda b, p, pt, g=g: (pt[b, n_pages - 1 - (p * group + g)], 0, 0))
           for g in range(group)]
    grid_spec = pltpu.PrefetchScalarGridSpec(
        num_scalar_prefetch=1,
        grid=(bsz, n_pages // group),
        in_specs=[new, new, new] + pgs + pgs,
        out_specs=new,
        scratch_shapes=[pltpu.VMEM((nq, d), BF16), pltpu.VMEM((nq, d), F32), pltpu.VMEM((nq, page), F32),
                        pltpu.VMEM((page, d), F32), pltpu.VMEM((page, d), F32)],
    )
    return pl.pallas_call(
        functools.partial(_paged_kernel, tq=tq, n_heads=n_heads, page=page, group=group),
        grid_spec=grid_spec,
        out_shape=jax.ShapeDtypeStruct((bsz, tq, d), F32),
        compiler_params=_params(("arbitrary", "arbitrary")),
        name="sb_paged",
    )(page_table, q, k_new, v_new, *([cache_k] * group), *([cache_v] * group))


def _proj_res_kernel(x_ref, res_ref, w_ref, out_ref):
    out_ref[...] = res_ref[...] + _dot(_bf(x_ref[...]), w_ref[...])


def _proj_res(x, res, w, *, tm):
    n, d = x.shape
    row = pl.BlockSpec((tm, d), lambda i: (i, 0))
    return pl.pallas_call(
        _proj_res_kernel,
        grid=(n // tm,),
        in_specs=[row, row, _const_spec(w.shape)],
        out_specs=row,
        out_shape=jax.ShapeDtypeStruct((n, d), F32),
        compiler_params=_params(("arbitrary",)),
        name="sb_out",
    )(x, res, w)


def _weights(norm_mix, norm_ffn, norm_final, tm_mu, tm_w_r, tm_w_k, tm_w_v, tm_w_o, tm_w0, tm_w_w1, tm_w_w2,
             tm_a0, tm_w_a1, tm_w_a2, tm_w_g1, tm_w_g2, tm_k_k, tm_k_a, tm_r_k, tm_gn_w, tm_gn_b,
             sb_w_qkv, sb_b_qkv, sb_w_o, ffn_w_gate, ffn_w_up, ffn_conv_w, ffn_conv_b, ffn_w_down):
    row = lambda a: a.reshape(1, -1)
    tm = dict(gmix=row(norm_mix[0]), mu=tm_mu[0], wr=_bf(tm_w_r[0]), wk=_bf(tm_w_k[0]), wv=_bf(tm_w_v[0]),
              w1=_bf(tm_w_w1[0]), w2=_bf(tm_w_w2[0]), w0=row(tm_w0[0]), a1=_bf(tm_w_a1[0]), a2=_bf(tm_w_a2[0]),
              a0=row(tm_a0[0]), g1=_bf(tm_w_g1[0]), g2=_bf(tm_w_g2[0]), k_k=row(tm_k_k[0]), k_a=row(tm_k_a[0]),
              gn_w=row(tm_gn_w[0]), gn_b=row(tm_gn_b[0]), r_k=row(tm_r_k[0]), wo=_bf(tm_w_o[0]))
    d = norm_final.shape[0]
    wqkv, bqkv = sb_w_qkv[0], sb_b_qkv[0]
    qv_cols = lambda a: jnp.concatenate([a[..., :d], a[..., 2 * d:]], axis=-1)
    sb = dict(gn=row(norm_mix[1]), wqkv=_bf(wqkv), bqkv=row(bqkv), wo=_bf(sb_w_o[0]),
              wqv=_bf(qv_cols(wqkv)), bqv=row(qv_cols(bqkv)), wkt=_bf(wqkv[:, d:2 * d].T),
              bk_col=bqkv[d:2 * d].reshape(d, 1))
    ffn = [dict(gn=row(norm_ffn[i]), wg=_bf(ffn_w_gate[i]), wu=_bf(ffn_w_up[i]), cw=ffn_conv_w[i],
                cb=row(ffn_conv_b[i]), wd=_bf(ffn_w_down[i])) for i in range(2)]
    return tm, sb, ffn, row(norm_final)


def _trunk_prompt(x, w_tm, w_sb, w_ffn, g_final, *, tm, chunk, qb, kg):
    bsz, t, d = x.shape
    n = bsz * t
    f = w_ffn[0]['wg'].shape[1]
    flat = lambda a: a.reshape(n, d)
    r, lw, kh, v, kk, kka, g, shift = _tm_pre(x, jnp.zeros((bsz, 1, d), F32), w_tm, step=1, tm=tm)
    s0 = jnp.zeros((bsz, d // HEAD_DIM, HEAD_DIM, HEAD_DIM), F32)
    y, wkv = _wkv(r, lw, kh, v, kk, kka, s0, c=chunk)
    tm_mid = min(DENSE_TILE, t)
    tm_out = min(OUT_PROJ_TILE, t)
    h = _tm_post(flat(y), flat(r), flat(kh), flat(v), flat(g), flat(x), w_tm, tm=tm_mid)
    conv0 = jnp.zeros((bsz, CONV_W - 1, f), F32)
    h, conv_a = _ffn(h.reshape(bsz, t, d), conv0, w_ffn[0], g_final, step=1, tm=tm, final_norm=False)
    q, kt, v2 = _qkv_kt(h, w_sb, tm=tm_mid)
    o = _attn_prompt(q, kt, v2, qb=qb, kg=kg)
    h = _proj_res(flat(o), flat(h), w_sb['wo'], tm=tm_out)
    out, conv_b = _ffn(h.reshape(bsz, t, d), conv0, w_ffn[1], g_final, step=1, tm=tm, final_norm=True)
    n_heads = d // HEAD_DIM
    k_new = jnp.transpose(kt.reshape(bsz, n_heads, HEAD_DIM, t), (0, 3, 1, 2))
    return (out, shift.reshape(1, bsz, d), wkv[None], k_new[None], v2.reshape(bsz, t, n_heads, HEAD_DIM)[None],
            jnp.stack([conv_a, conv_b]))


def _trunk_sample(x, state_shift, state_wkv, state_conv, cache_k, cache_v, page_table, w_tm, w_sb, w_ffn, g_final,
                  *, chunk):
    bsz, t, d = x.shape
    n = bsz * t
    f = w_ffn[0]['wg'].shape[1]
    to_tm = lambda a: jnp.swapaxes(a, 0, 1).reshape(1, n, a.shape[-1])
    to_bm = lambda a: jnp.swapaxes(a.reshape(t, bsz, a.shape[-1]), 0, 1)
    xs = to_tm(x)
    r, lw, kh, v, kk, kka, g, shift = _tm_pre(xs, state_shift[0][None], w_tm, step=bsz, tm=n)
    pad_t = lambda a: jnp.pad(to_bm(a), ((0, 0), (0, chunk - t), (0, 0)))
    y, wkv = _wkv(pad_t(r), pad_t(lw), pad_t(kh), pad_t(v), pad_t(kk), pad_t(kka), state_wkv[0], c=chunk)
    y = to_tm(y[:, :t])
    flat = lambda a: a.reshape(n, d)
    h = _tm_post(flat(y), flat(r), flat(kh), flat(v), flat(g), flat(xs), w_tm, tm=n)
    conv_tm = lambda a: jnp.swapaxes(a, 0, 1).reshape(1, (CONV_W - 1) * bsz, f)
    conv_bm = lambda a: jnp.swapaxes(a.reshape(CONV_W - 1, bsz, f), 0, 1)
    h, conv_a = _ffn(h.reshape(1, n, d), conv_tm(state_conv[0]), w_ffn[0], g_final, step=bsz, tm=n,
                     final_norm=False)
    q, k, v2 = _qkv(flat(h), w_sb, tm=n)
    q, k, v2 = to_bm(q), to_bm(k), to_bm(v2)
    n_phys, page = cache_k.shape[1], cache_k.shape[2]
    keys_minor = lambda c: jnp.transpose(c[0], (0, 2, 3, 1)).reshape(n_phys, d, page)
    o = _attn_paged(q, k, v2, keys_minor(cache_k), keys_minor(cache_v), page_table)
    h = _proj_res(flat(to_tm(o)), flat(h), w_sb['wo'], tm=n)
    out, conv_b = _ffn(h.reshape(1, n, d), conv_tm(state_conv[1]), w_ffn[1], g_final, step=bsz, tm=n,
                       final_norm=True)
    hd = (bsz, t, d // HEAD_DIM, HEAD_DIM)
    return (to_bm(out[0]), shift.reshape(1, bsz, d), wkv[None], k.reshape(hd)[None], v2.reshape(hd)[None],
            jnp.stack([conv_bm(conv_a[0]), conv_bm(conv_b[0])]))


PROMPT_TILE = 256
DENSE_TILE = 512
OUT_PROJ_TILE = 1024
WKV_TILES_PER_STEP = 8
WKV_CHUNK = 64
WKV_INVERSE_BASE = 8
SAMPLE_WKV_CHUNK = 8
Q_BLOCK = 128
KEY_GROUP = 512
ATTN_QUERY_BLOCKS_PER_BODY = 4
PAGES_PER_STEP = 8


def kernel(x_prompt, x_sample, state_shift, state_wkv, cache_k, cache_v, state_conv, page_table, norm_mix, norm_ffn, norm_final, tm_mu, tm_w_r, tm_w_k, tm_w_v, tm_w_o, tm_w0, tm_w_w1, tm_w_w2, tm_a0, tm_w_a1, tm_w_a2, tm_w_g1, tm_w_g2, tm_k_k, tm_k_a, tm_r_k, tm_gn_w, tm_gn_b, sb_w_qkv, sb_b_qkv, sb_w_o, ffn_w_gate, ffn_w_up, ffn_conv_w, ffn_conv_b, ffn_w_down):
    w_tm, w_sb, w_ffn, g_final = _weights(
        norm_mix, norm_ffn, norm_final, tm_mu, tm_w_r, tm_w_k, tm_w_v, tm_w_o, tm_w0, tm_w_w1, tm_w_w2,
        tm_a0, tm_w_a1, tm_w_a2, tm_w_g1, tm_w_g2, tm_k_k, tm_k_a, tm_r_k, tm_gn_w, tm_gn_b,
        sb_w_qkv, sb_b_qkv, sb_w_o, ffn_w_gate, ffn_w_up, ffn_conv_w, ffn_conv_b, ffn_w_down)
    tile = min(PROMPT_TILE, x_prompt.shape[1])
    y_p, shift_p, wkv_p, k_p, v_p, conv_p = _trunk_prompt(
        x_prompt, w_tm, w_sb, w_ffn, g_final, tm=tile, chunk=min(WKV_CHUNK, x_prompt.shape[1]),
        qb=min(Q_BLOCK, x_prompt.shape[1]), kg=min(KEY_GROUP, x_prompt.shape[1]))
    y_s, shift_s, wkv_s, k_s, v_s, conv_s = _trunk_sample(
        x_sample, state_shift, state_wkv, state_conv, cache_k, cache_v, page_table, w_tm, w_sb, w_ffn, g_final,
        chunk=SAMPLE_WKV_CHUNK)
    return (y_p, y_s, shift_p, wkv_p, k_p, v_p, conv_p, shift_s, wkv_s, k_s, v_s, conv_s)
```

```python
import functools
import math

import jax
import jax.numpy as jnp
from jax import lax
from jax.experimental import pallas as pl
from jax.experimental.pallas import tpu as pltpu

HEAD_DIM = 64
LANES = 128
SUBLANES = 8
HEADS_PER_TILE = LANES // HEAD_DIM
CONV_W = 3
RMS_EPS = 1e-6
GN_EPS = 64e-5
KK_EPS = 1e-12
N_MIX = 6
VMEM_LIMIT = 56 * 1024 * 1024

F32 = jnp.float32
BF16 = jnp.bfloat16


def _bf(x):
    return x.astype(BF16)


def _indicator(mask, dtype=F32):
    return jnp.where(mask, 1.0, 0.0).astype(dtype)


def _dot(a, b):
    return jnp.dot(a, b, preferred_element_type=F32)


def _dot_nt(a, b):
    return lax.dot_general(a, b, (((1,), (1,)), ((), ())), preferred_element_type=F32)


def _dot_tn(a, b):
    return lax.dot_general(a, b, (((0,), (0,)), ((), ())), preferred_element_type=F32)


def _split2(x):
    hi = x.astype(BF16)
    lo = (x - hi.astype(F32)).astype(BF16)
    return hi, lo


def _split3(x):
    hi = x.astype(BF16)
    r1 = x - hi.astype(F32)
    mid = r1.astype(BF16)
    lo = (r1 - mid.astype(F32)).astype(BF16)
    return hi, mid, lo


def _dot_sel(x, m):
    hi, lo = _split2(x)
    if x.shape[-1] == LANES:
        return _dot(jnp.concatenate([hi, lo], axis=1), jnp.concatenate([m, m], axis=0))
    return _dot(hi, m) + _dot(lo, m)


def _dot_hp(a, b, dot=_dot):
    ah, al = _split2(a)
    bh, bl = _split2(b)
    ka = 0 if dot is _dot_tn else 1
    kb = 1 if dot is _dot_nt else 0
    if a.shape[ka] == LANES:
        return dot(jnp.concatenate([ah, al], axis=ka), jnp.concatenate([bh, bh], axis=kb)) + dot(ah, bl)
    return dot(ah, bh) + (dot(ah, bl) + dot(al, bh))


def _rms(x, g):
    return x * lax.rsqrt(jnp.mean(x * x, axis=-1, keepdims=True) + RMS_EPS) * g


def _softplus(x):
    return jnp.maximum(x, 0.0) + jnp.log(1.0 + jnp.exp(-jnp.abs(x)))


def _sigmoid(x):
    return 1.0 / (1.0 + jnp.exp(-x))


def _head_ones():
    r = lax.broadcasted_iota(jnp.int32, (LANES, LANES), 0) // HEAD_DIM
    c = lax.broadcasted_iota(jnp.int32, (LANES, LANES), 1) // HEAD_DIM
    return _indicator(r == c, BF16)


def _head_sum(x, ones_bd):
    d = x.shape[-1]
    cols = [_dot_sel(x[:, g * LANES:(g + 1) * LANES], ones_bd) for g in range(d // LANES)]
    return jnp.concatenate(cols, axis=1)


def _interleave(stage_generators):
    live = list(stage_generators)
    done = object()
    while live:
        live = [g for g in live if next(g, done) is not done]


def _const_spec(shape):
    nd = len(shape)
    return pl.BlockSpec(shape, lambda *_: (0,) * nd, pipeline_mode=pl.Buffered(1))


def _params(sem):
    return pltpu.CompilerParams(dimension_semantics=sem, vmem_limit_bytes=VMEM_LIMIT)


def _tm_pre_kernel(x_ref, init_ref, gmix_ref, mu_ref, wr_ref, wk_ref, wv_ref, w1_ref, w2_ref, w0_ref,
                   a1_ref, a2_ref, a0_ref, g1_ref, g2_ref, kk_ref, ka_ref,
                   r_out, lw_out, kh_out, v_out, kk_out, kka_out, g_out, shift_out, hbuf, *, step, pad, tm):
    t = pl.program_id(1)
    hn = _rms(x_ref[0], gmix_ref[...])

    @pl.when(t == 0)
    def _():
        hbuf[pad - step:pad, :] = init_ref[0]

    @pl.when(t > 0)
    def _():
        hbuf[pad - step:pad, :] = hbuf[pad + tm - step:pad + tm, :]

    hbuf[pad:pad + tm, :] = hn
    xx = hbuf[pad - step:pad - step + tm, :] - hn
    mu = mu_ref[...]
    xr, xw, xk, xv, xa, xg = [_bf(hn + xx * mu[i:i + 1, :]) for i in range(N_MIX)]
    r = _dot(xr, wr_ref[...])
    k = _dot(xk, wk_ref[...])
    v = _dot(xv, wv_ref[...])
    wl = w0_ref[...] + _dot(_bf(jnp.tanh(_dot(xw, w1_ref[...]))), w2_ref[...])
    w_log = -_softplus(-wl) - 0.5
    a = _sigmoid(a0_ref[...] + _dot(_bf(_dot(xa, a1_ref[...])), a2_ref[...]))
    g = _dot(_bf(_sigmoid(_dot(xg, g1_ref[...]))), g2_ref[...])
    kk = k * kk_ref[...]
    nrm = jnp.sqrt(_head_sum(kk * kk, _head_ones()))
    kk = kk / jnp.maximum(nrm, KK_EPS)
    r_out[0] = r
    lw_out[0] = -jnp.exp(w_log)
    kh_out[0] = k * (1.0 + (a - 1.0) * ka_ref[...])
    v_out[0] = v
    kk_out[0] = kk
    kka_out[0] = kk * a
    g_out[0] = g
    shift_out[0] = hn[tm - step:, :]


def _tm_pre(x, init, w, *, step, tm):
    nb, tt, d = x.shape
    pad = max(SUBLANES, step)
    nt = tt // tm
    row = pl.BlockSpec((1, tm, d), lambda b, t: (b, t, 0))
    weights = [w['gmix'], w['mu'], w['wr'], w['wk'], w['wv'], w['w1'], w['w2'], w['w0'],
               w['a1'], w['a2'], w['a0'], w['g1'], w['g2'], w['k_k'], w['k_a']]
    outs = pl.pallas_call(
        functools.partial(_tm_pre_kernel, step=step, pad=pad, tm=tm),
        grid=(nb, nt),
        in_specs=[row, pl.BlockSpec((1, step, d), lambda b, t: (b, 0, 0))] + [_const_spec(a.shape) for a in weights],
        out_specs=[row] * 7 + [pl.BlockSpec((1, step, d), lambda b, t: (b, 0, 0))],
        out_shape=[jax.ShapeDtypeStruct((nb, tt, d), F32)] * 7 + [jax.ShapeDtypeStruct((nb, step, d), F32)],
        scratch_shapes=[pltpu.VMEM((pad + tm, d), F32)],
        compiler_params=_params(("arbitrary", "arbitrary")),
        name="tm_pre",
    )(x, init, *weights)
    return outs


def _wkv_kernel(r_ref, lw_ref, k_ref, v_ref, kk_ref, kka_ref, s0_ref, y_ref, sT_ref, s_scr, *, c, tiles):
    ci = pl.program_id(2)
    c2 = 2 * c
    lane_head = lax.broadcasted_iota(jnp.int32, (1, LANES), 1) // HEAD_DIM
    m0 = _indicator(lane_head == 0)
    m1 = 1.0 - m0
    hr = lax.broadcasted_iota(jnp.int32, (LANES, LANES), 0) // HEAD_DIM
    hc = lax.broadcasted_iota(jnp.int32, (LANES, LANES), 1) // HEAD_DIM
    same_head = hr == hc
    trow = lax.broadcasted_iota(jnp.int32, (c, c), 0)
    tcol = lax.broadcasted_iota(jnp.int32, (c, c), 1)
    lower = _indicator(tcol <= trow, BF16)
    rr = lax.broadcasted_iota(jnp.int32, (c2, c2), 0)
    cc = lax.broadcasted_iota(jnp.int32, (c2, c2), 1)
    ti = rr % c
    tj = cc % c
    top = rr < c
    left = cc < c
    keep0 = tj < ti + jnp.where(top, 0, 1)
    keep1 = tj < ti + jnp.where(top, 1, 0)
    left_half = lax.broadcasted_iota(jnp.int32, (c, c2), 1) < c
    wide = c2 % LANES == 0
    eye = _indicator(rr == cc)
    base = min(WKV_INVERSE_BASE, c)
    n_base = int(math.log2(base))

    @pl.when(ci == 0)
    def _():
        s_scr[...] = s0_ref[0]

    def tile_stages(tile):
        lanes = slice(tile * LANES, (tile + 1) * LANES)
        r = r_ref[0, :, lanes]
        lw = lw_ref[0, :, lanes]
        k = k_ref[0, :, lanes]
        v = v_ref[0, :, lanes]
        a = -kk_ref[0, :, lanes]
        b = kka_ref[0, :, lanes]
        s0 = s_scr[tile]

        h1, h2, h3 = _split3(lw)
        cum = _dot(lower, h1) + (_dot(lower, h2) + _dot(lower, h3))
        yield
        cum_c = cum[c - 1:c, :]
        a_t = a * jnp.exp(cum - lw)
        r_t = r * jnp.exp(cum)
        einv = jnp.exp(-cum)
        b_t = b * einv
        k_t = k * einv
        eend = jnp.exp(cum_c - cum)
        b_e = b * eend
        k_e = k * eend

        sc0 = _dot_hp(jnp.concatenate([a_t * m0, r_t * m0], axis=0), jnp.concatenate([b_t, k_t], axis=0), _dot_nt)
        sc1 = _dot_hp(jnp.concatenate([r_t * m1, a_t * m1], axis=0), jnp.concatenate([k_t, b_t], axis=0), _dot_nt)
        ar_s = _dot_hp(jnp.concatenate([a_t, r_t], axis=0), s0, _dot_nt)
        yield
        sc0 = jnp.where(keep0, sc0, 0.0)
        sc1 = jnp.where(keep1, sc1, 0.0)
        v0 = v * m0
        v1 = v * m1
        ak = jnp.where(left_half, sc1[c:], sc0[:c])
        rhs = ar_s[:c] + _dot_hp(ak, jnp.concatenate([v1, v0], axis=0))
        yield

        n_bd = jnp.where(top & left, sc0, 0.0) + jnp.where((~top) & (~left), sc1, 0.0)
        p = jnp.where(ti // base == tj // base, n_bd, 0.0)
        t_inv = eye + p
        if n_base > 1:
            p = _dot_hp(p, p)
            yield
        for i in range(1, n_base):
            if i == n_base - 1:
                t_inv = t_inv + _dot_hp(p, t_inv)
            elif wide:
                pt = _dot_hp(p, jnp.concatenate([p, t_inv], axis=1))
                p, t_inv = pt[:, :c2], t_inv + pt[:, c2:]
            else:
                t_inv = t_inv + _dot_hp(p, t_inv)
                yield
                p = _dot_hp(p, p)
            yield
        m = base
        while m < c:
            n_off = jnp.where((ti // (2 * m) == tj // (2 * m)) & (ti // m != tj // m), n_bd, 0.0)
            nt = _dot_hp(n_off, t_inv)
            yield
            t_inv = t_inv + _dot_hp(t_inv, nt)
            yield
            m *= 2
        x = _dot_hp(t_inv, jnp.concatenate([rhs * m0, rhs * m1], axis=0))
        yield
        u0 = x[:c]
        u1 = x[c:]

        if wide:
            y_uv = _dot_hp(jnp.concatenate([sc0[c:], sc1[:c]], axis=1), jnp.concatenate([u0, v0, v1, u1], axis=0))
        else:
            y_uv = (_dot_hp(sc0[c:], jnp.concatenate([u0, v0], axis=0))
                    + _dot_hp(sc1[:c], jnp.concatenate([v1, u1], axis=0)))
        upd = _dot_hp(jnp.concatenate([u0 + u1, v], axis=0), jnp.concatenate([b_e, k_e], axis=0), _dot_tn)
        yield
        y_ref[0, :, lanes] = ar_s[c:] + y_uv
        s_scr[tile] = s0 * jnp.exp(cum_c) + jnp.where(same_head, upd, 0.0)

    _interleave(tile_stages(tile) for tile in range(tiles))

    @pl.when(ci == pl.num_programs(2) - 1)
    def _():
        sT_ref[0] = s_scr[...]


def _wkv(r, lw, kh, v, kk, kka, s0, *, c):
    bsz, t, d = r.shape
    n_pairs = d // LANES
    assert HEADS_PER_TILE == 2
    s0_pairs = s0.reshape(bsz, n_pairs, HEADS_PER_TILE, HEAD_DIM, HEAD_DIM)
    zero = jnp.zeros((bsz, n_pairs, HEAD_DIM, HEAD_DIM), F32)
    s0_bd = jnp.concatenate([jnp.concatenate([s0_pairs[:, :, 0], zero], axis=-1),
                             jnp.concatenate([zero, s0_pairs[:, :, 1]], axis=-1)], axis=-2)
    tiles = WKV_TILES_PER_STEP
    seq = pl.BlockSpec((1, c, tiles * LANES), lambda b, p, i: (b, i, p))
    st = pl.BlockSpec((1, tiles, LANES, LANES), lambda b, p, i: (b, p, 0, 0))
    y, s_bd = pl.pallas_call(
        functools.partial(_wkv_kernel, c=c, tiles=tiles),
        grid=(bsz, n_pairs // tiles, t // c),
        in_specs=[seq] * 6 + [st],
        out_specs=[seq, st],
        out_shape=[jax.ShapeDtypeStruct((bsz, t, d), F32), jax.ShapeDtypeStruct(s0_bd.shape, F32)],
        scratch_shapes=[pltpu.VMEM((tiles, LANES, LANES), F32)],
        compiler_params=_params(("arbitrary", "arbitrary", "arbitrary")),
        name="wkv",
    )(r, lw, kh, v, kk, kka, s0_bd)
    s_fin = jnp.stack([s_bd[:, :, :HEAD_DIM, :HEAD_DIM], s_bd[:, :, HEAD_DIM:, HEAD_DIM:]], axis=2)
    return y, s_fin.reshape(s0.shape)


def _tm_post_kernel(y_ref, r_ref, kh_ref, v_ref, g_ref, x_ref, gnw_ref, gnb_ref, rk_ref, wo_ref, out_ref):
    ones_bd = _head_ones()
    y = y_ref[...]
    mean = _head_sum(y, ones_bd) * (1.0 / HEAD_DIM)
    yc = y - mean
    var = _head_sum(yc * yc, ones_bd) * (1.0 / HEAD_DIM)
    yn = yc * lax.rsqrt(var + GN_EPS) * gnw_ref[...] + gnb_ref[...]
    bonus = _head_sum(r_ref[...] * kh_ref[...] * rk_ref[...], ones_bd) * v_ref[...]
    out_ref[...] = x_ref[...] + _dot(_bf((yn + bonus) * g_ref[...]), wo_ref[...])


def _tm_post(y, r, kh, v, g, x, w, *, tm):
    n, d = x.shape
    row = pl.BlockSpec((tm, d), lambda i: (i, 0))
    weights = [w['gn_w'], w['gn_b'], w['r_k'], w['wo']]
    return pl.pallas_call(
        _tm_post_kernel,
        grid=(n // tm,),
        in_specs=[row] * 6 + [_const_spec(a.shape) for a in weights],
        out_specs=row,
        out_shape=jax.ShapeDtypeStruct((n, d), F32),
        compiler_params=_params(("arbitrary",)),
        name="tm_post",
    )(y, r, kh, v, g, x, *weights)


def _ffn_kernel(h_ref, init_ref, gn_ref, wg_ref, wu_ref, cw_ref, cb_ref, wd_ref, gfin_ref,
                out_ref, conv_out, abuf, *, step, pad, tm, final_norm):
    t = pl.program_id(1)
    hist = (CONV_W - 1) * step
    h = h_ref[0]
    hb = _bf(_rms(h, gn_ref[...]))

    @pl.when(t == 0)
    def _():
        abuf[pad - hist:pad, :] = init_ref[0]

    @pl.when(t > 0)
    def _():
        abuf[pad - hist:pad, :] = abuf[pad + tm - hist:pad + tm, :]

    a = _dot(hb, wg_ref[...])
    abuf[pad:pad + tm, :] = a
    cw = cw_ref[...]
    conv = cb_ref[...] + a * cw[CONV_W - 1:CONV_W, :]
    for i in range(CONV_W - 1):
        off = pad - (CONV_W - 1 - i) * step
        conv = conv + abuf[off:off + tm, :] * cw[i:i + 1, :]
    u = _dot(hb, wu_ref[...])
    act = conv * _sigmoid(conv) * u
    out = h + _dot(_bf(act), wd_ref[...])
    if final_norm:
        out = _rms(out, gfin_ref[...])
    out_ref[0] = out
    conv_out[0] = abuf[pad + tm - hist:pad + tm, :]


def _ffn(h, init, w, gfin, *, step, tm, final_norm):
    nb, tt, d = h.shape
    f = w['wg'].shape[1]
    hist = (CONV_W - 1) * step
    pad = max(SUBLANES, hist)
    row = pl.BlockSpec((1, tm, d), lambda b, t: (b, t, 0))
    cst = pl.BlockSpec((1, hist, f), lambda b, t: (b, 0, 0))
    weights = [w['gn'], w['wg'], w['wu'], w['cw'], w['cb'], w['wd'], gfin]
    return pl.pallas_call(
        functools.partial(_ffn_kernel, step=step, pad=pad, tm=tm, final_norm=final_norm),
        grid=(nb, tt // tm),
        in_specs=[row, cst] + [_const_spec(a.shape) for a in weights],
        out_specs=[row, cst],
        out_shape=[jax.ShapeDtypeStruct((nb, tt, d), F32), jax.ShapeDtypeStruct((nb, hist, f), F32)],
        scratch_shapes=[pltpu.VMEM((pad + tm, f), F32)],
        compiler_params=_params(("arbitrary", "arbitrary")),
        name="ffn_final" if final_norm else "ffn",
    )(h, init, *weights)


def _qkv_kernel(h_ref, gn_ref, w_ref, b_ref, q_out, k_out, v_out):
    d = h_ref.shape[-1]
    hb = _bf(_rms(h_ref[...], gn_ref[...]))
    qkv = _dot(hb, w_ref[...]) + b_ref[...]
    q_out[...] = qkv[:, :d]
    k_out[...] = qkv[:, d:2 * d]
    v_out[...] = qkv[:, 2 * d:]


def _qkv(h, w, *, tm):
    n, d = h.shape
    row = pl.BlockSpec((tm, d), lambda i: (i, 0))
    weights = [w['gn'], w['wqkv'], w['bqkv']]
    return pl.pallas_call(
        _qkv_kernel,
        grid=(n // tm,),
        in_specs=[row] + [_const_spec(a.shape) for a in weights],
        out_specs=[row] * 3,
        out_shape=[jax.ShapeDtypeStruct((n, d), F32)] * 3,
        compiler_params=_params(("arbitrary",)),
        name="sb_qkv",
    )(h, *weights)


def _qkv_kt_kernel(h_ref, gn_ref, w_ref, b_ref, wkt_ref, bk_ref, q_out, kt_out, v_out):
    d = h_ref.shape[-1]
    hb = _bf(_rms(h_ref[0], gn_ref[...]))
    qv = _dot(hb, w_ref[...]) + b_ref[...]
    q_out[0] = qv[:, :d]
    v_out[0] = qv[:, d:]
    kt_out[0] = _dot_nt(wkt_ref[...], hb) + bk_ref[...]


def _qkv_kt(h, w, *, tm):
    bsz, t, d = h.shape
    row = pl.BlockSpec((1, tm, d), lambda b, i: (b, i, 0))
    col = pl.BlockSpec((1, d, tm), lambda b, i: (b, 0, i))
    weights = [w['gn'], w['wqv'], w['bqv'], w['wkt'], w['bk_col']]
    return pl.pallas_call(
        _qkv_kt_kernel,
        grid=(bsz, t // tm),
        in_specs=[row] + [_const_spec(a.shape) for a in weights],
        out_specs=[row, col, row],
        out_shape=[jax.ShapeDtypeStruct((bsz, t, d), F32), jax.ShapeDtypeStruct((bsz, d, t), F32),
                   jax.ShapeDtypeStruct((bsz, t, d), F32)],
        compiler_params=_params(("arbitrary", "arbitrary")),
        name="sb_qkv_kt",
    )(h, *weights)


def _suffix_ones(n):
    r = lax.broadcasted_iota(jnp.int32, (n, 2 * n), 0)
    c = lax.broadcasted_iota(jnp.int32, (n, 2 * n), 1)
    return _indicator((r >= c) | (c >= n), BF16)


def _stick_stages(qm, kb, vb, valid, tri, run, keys_minor=False):
    nk = tri.shape[0]
    z = _dot(qm, kb) if keys_minor else _dot_nt(qm, kb)
    yield
    drop = _softplus(z)
    if valid is not None:
        drop = jnp.where(valid, drop, 0.0)
    cs = _dot_sel(drop, tri)
    yield
    att = jnp.exp(z - (cs[:, :nk] + run['carry']))
    run['carry'] = run['carry'] + cs[:, nk:]
    if valid is not None:
        att = jnp.where(valid, att, 0.0)
    pv = _dot_nt(_bf(att), vb) if keys_minor else _dot(_bf(att), vb)
    yield
    run['acc'] = run['acc'] + pv


def _attn_kernel(q_ref, kt_ref, v_ref, o_ref, acc_ref, carry_ref, ktb, vgb, *, qb, kg, qu):
    nsub = kg // qb
    t = q_ref.shape[1]
    n_groups = t // kg
    n_qblocks = t // qb
    lane_head = lax.broadcasted_iota(jnp.int32, (1, LANES), 1) // HEAD_DIM
    tri = _suffix_ones(qb)

    def chain(qi, h, nblk, own_group):
        width = nblk * qb
        q = q_ref[0, pl.ds(pl.multiple_of(qi * qb, qb), qb), :] * (HEAD_DIM ** -0.5)
        z = _dot(_bf(jnp.where(lane_head == h, q, 0.0)), ktb[:, :width])
        yield
        drop = _softplus(z)
        if own_group:
            row = lax.broadcasted_iota(jnp.int32, (qb, width), 0)
            col = lax.broadcasted_iota(jnp.int32, (qb, width), 1)
            valid = col < row + (nblk - 1) * qb
            drop = jnp.where(valid, drop, 0.0)
        hi, lo = _split2(drop)
        subs = [slice(s * qb, (s + 1) * qb) for s in range(nblk)]
        tri2 = jnp.concatenate([tri, tri], axis=0)
        cs = [_dot(jnp.concatenate([hi[:, sub], lo[:, sub]], axis=1), tri2) for sub in subs]
        yield
        run = jnp.zeros((qb, qb), F32) if own_group else carry_ref[qi, h]
        later = [None] * nblk
        for s in reversed(range(nblk)):
            later[s] = cs[s][:, :qb] + run
            run = run + cs[s][:, qb:]
        carry_ref[qi, h] = run
        att = jnp.exp(z - jnp.concatenate(later, axis=1))
        if own_group:
            att = jnp.where(valid, att, 0.0)
        pv = _dot(_bf(att), vgb[:width, :])
        yield
        acc_ref[qi, h] = pv if own_group else acc_ref[qi, h] + pv

    heads = range(HEADS_PER_TILE)

    def key_group(jj, carry):
        j = n_groups - 1 - jj
        start = pl.multiple_of(j * kg, kg)
        ktb[...] = _bf(kt_ref[0, :, pl.ds(start, kg)])
        vgb[...] = _bf(v_ref[0, pl.ds(start, kg), :])
        for r0 in range(0, nsub, qu):
            _interleave(chain(j * nsub + r, h, r + 1, True) for r in range(r0, r0 + qu) for h in heads)

        def later_queries(it, c):
            q0 = (j + 1) * nsub + it * qu
            _interleave(chain(q0 + u, h, nsub, False) for u in range(qu) for h in heads)
            return c

        lax.fori_loop(0, (n_groups - 1 - j) * (nsub // qu), later_queries, 0)
        return carry

    lax.fori_loop(0, n_groups, key_group, 0)

    def write_out(qi, carry):
        rows = pl.ds(pl.multiple_of(qi * qb, qb), qb)
        o_ref[0, rows, :] = jnp.where(lane_head == 0, acc_ref[qi, 0], acc_ref[qi, 1])
        return carry

    lax.fori_loop(0, n_qblocks, write_out, 0)


def _attn_prompt(q, kt, v, *, qb, kg):
    bsz, t, d = q.shape
    n_pairs = d // LANES
    qu = math.gcd(ATTN_QUERY_BLOCKS_PER_BODY, kg // qb)
    seq = pl.BlockSpec((1, t, LANES), lambda b, p: (b, 0, p))
    ks = pl.BlockSpec((1, LANES, t), lambda b, p: (b, p, 0))
    return pl.pallas_call(
        functools.partial(_attn_kernel, qb=qb, kg=kg, qu=qu),
        grid=(bsz, n_pairs),
        in_specs=[seq, ks, seq],
        out_specs=seq,
        out_shape=jax.ShapeDtypeStruct((bsz, t, d), F32),
        scratch_shapes=[pltpu.VMEM((t // qb, HEADS_PER_TILE, qb, LANES), F32),
                        pltpu.VMEM((t // qb, HEADS_PER_TILE, qb, qb), F32),
                        pltpu.VMEM((LANES, kg), BF16), pltpu.VMEM((kg, LANES), BF16)],
        compiler_params=_params(("arbitrary", "arbitrary")),
        name="sb_attn",
    )(q, kt, v)


def _paged_kernel(pt_ref, q_ref, kn_ref, vn_ref, *refs, tq, n_heads, page, group):
    kc_refs = refs[:group]
    vc_refs = refs[group:2 * group]
    o_ref, qexp, acc_ref, carry_ref, kpad, vpad = refs[2 * group:]
    p = pl.program_id(1)
    d = q_ref.shape[-1]
    nq = tq * n_heads
    tri = _suffix_ones(page)
    head_of_lane = lax.broadcasted_iota(jnp.int32, (n_heads, d), 1) // HEAD_DIM
    head_mask = head_of_lane == lax.broadcasted_iota(jnp.int32, (n_heads, d), 0)

    @pl.when(p == 0)
    def _():
        q = q_ref[0] * (HEAD_DIM ** -0.5)
        for t in range(tq):
            qexp[t * n_heads:(t + 1) * n_heads, :] = _bf(jnp.where(head_mask, q[t:t + 1, :], 0.0))
        kpad[...] = jnp.zeros(kpad.shape, F32)
        vpad[...] = jnp.zeros(vpad.shape, F32)
        kpad[0:tq, :] = kn_ref[0]
        vpad[0:tq, :] = vn_ref[0]
        row_t = lax.broadcasted_iota(jnp.int32, (nq, page), 0) // n_heads
        col = lax.broadcasted_iota(jnp.int32, (nq, page), 1)
        run = dict(carry=jnp.zeros((nq, page), F32), acc=jnp.zeros((nq, d), F32))
        _interleave([_stick_stages(qexp[...], _bf(kpad[...]), _bf(vpad[...]), col < row_t, tri, run)])
        acc_ref[...] = run['acc']
        carry_ref[...] = run['carry']

    run = dict(carry=carry_ref[...], acc=acc_ref[...])
    _interleave(_stick_stages(qexp[...], _bf(kc_ref[0]), _bf(vc_ref[0]), None, tri, run, keys_minor=True)
                for kc_ref, vc_ref in zip(kc_refs, vc_refs))
    acc = run['acc']
    acc_ref[...] = acc
    carry_ref[...] = run['carry']

    @pl.when(p == pl.num_programs(1) - 1)
    def _():
        rows = [jnp.sum(jnp.where(head_mask, acc[t * n_heads:(t + 1) * n_heads, :], 0.0), axis=0, keepdims=True)
                for t in range(tq)]
        o_ref[0] = jnp.concatenate(rows, axis=0)


def _attn_paged(q, k_new, v_new, cache_k, cache_v, page_table):
    bsz, tq, d = q.shape
    n_heads = d // HEAD_DIM
    n_pages = page_table.shape[1]
    page = cache_k.shape[2]
    nq = tq * n_heads
    group = math.gcd(PAGES_PER_STEP, n_pages)
    new = pl.BlockSpec((1, tq, d), lambda b, p, pt: (b, 0, 0))
    pgs = [pl.BlockSpec((1, d, page), lambda b, p, pt, g=g: (pt[b, n_pages - 1 - (p * group + g)], 0, 0))
           for g in range(group)]
    grid_spec = pltpu.PrefetchScalarGridSpec(
        num_scalar_prefetch=1,
        grid=(bsz, n_pages // group),
        in_specs=[new, new, new] + pgs + pgs,
        out_specs=new,
        scratch_shapes=[pltpu.VMEM((nq, d), BF16), pltpu.VMEM((nq, d), F32), pltpu.VMEM((nq, page), F32),
                        pltpu.VMEM((page, d), F32), pltpu.VMEM((page, d), F32)],
    )
    return pl.pallas_call(
        functools.partial(_paged_kernel, tq=tq, n_heads=n_heads, page=page, group=group),
        grid_spec=grid_spec,
        out_shape=jax.ShapeDtypeStruct((bsz, tq, d), F32),
        compiler_params=_params(("arbitrary", "arbitrary")),
        name="sb_paged",
    )(page_table, q, k_new, v_new, *([cache_k] * group), *([cache_v] * group))


def _proj_res_kernel(x_ref, res_ref, w_ref, out_ref):
    out_ref[...] = res_ref[...] + _dot(_bf(x_ref[...]), w_ref[...])


def _proj_res(x, res, w, *, tm):
    n, d = x.shape
    row = pl.BlockSpec((tm, d), lambda i: (i, 0))
    return pl.pallas_call(
        _proj_res_kernel,
        grid=(n // tm,),
        in_specs=[row, row, _const_spec(w.shape)],
        out_specs=row,
        out_shape=jax.ShapeDtypeStruct((n, d), F32),
        compiler_params=_params(("arbitrary",)),
        name="sb_out",
    )(x, res, w)


def _weights(norm_mix, norm_ffn, norm_final, tm_mu, tm_w_r, tm_w_k, tm_w_v, tm_w_o, tm_w0, tm_w_w1, tm_w_w2,
             tm_a0, tm_w_a1, tm_w_a2, tm_w_g1, tm_w_g2, tm_k_k, tm_k_a, tm_r_k, tm_gn_w, tm_gn_b,
             sb_w_qkv, sb_b_qkv, sb_w_o, ffn_w_gate, ffn_w_up, ffn_conv_w, ffn_conv_b, ffn_w_down):
    row = lambda a: a.reshape(1, -1)
    tm = dict(gmix=row(norm_mix[0]), mu=tm_mu[0], wr=_bf(tm_w_r[0]), wk=_bf(tm_w_k[0]), wv=_bf(tm_w_v[0]),
              w1=_bf(tm_w_w1[0]), w2=_bf(tm_w_w2[0]), w0=row(tm_w0[0]), a1=_bf(tm_w_a1[0]), a2=_bf(tm_w_a2[0]),
              a0=row(tm_a0[0]), g1=_bf(tm_w_g1[0]), g2=_bf(tm_w_g2[0]), k_k=row(tm_k_k[0]), k_a=row(tm_k_a[0]),
              gn_w=row(tm_gn_w[0]), gn_b=row(tm_gn_b[0]), r_k=row(tm_r_k[0]), wo=_bf(tm_w_o[0]))
    d = norm_final.shape[0]
    wqkv, bqkv = sb_w_qkv[0], sb_b_qkv[0]
    qv_cols = lambda a: jnp.concatenate([a[..., :d], a[..., 2 * d:]], axis=-1)
    sb = dict(gn=row(norm_mix[1]), wqkv=_bf(wqkv), bqkv=row(bqkv), wo=_bf(sb_w_o[0]),
              wqv=_bf(qv_cols(wqkv)), bqv=row(qv_cols(bqkv)), wkt=_bf(wqkv[:, d:2 * d].T),
              bk_col=bqkv[d:2 * d].reshape(d, 1))
    ffn = [dict(gn=row(norm_ffn[i]), wg=_bf(ffn_w_gate[i]), wu=_bf(ffn_w_up[i]), cw=ffn_conv_w[i],
                cb=row(ffn_conv_b[i]), wd=_bf(ffn_w_down[i])) for i in range(2)]
    return tm, sb, ffn, row(norm_final)


def _trunk_prompt(x, w_tm, w_sb, w_ffn, g_final, *, tm, chunk, qb, kg):
    bsz, t, d = x.shape
    n = bsz * t
    f = w_ffn[0]['wg'].shape[1]
    flat = lambda a: a.reshape(n, d)
    r, lw, kh, v, kk, kka, g, shift = _tm_pre(x, jnp.zeros((bsz, 1, d), F32), w_tm, step=1, tm=tm)
    s0 = jnp.zeros((bsz, d // HEAD_DIM, HEAD_DIM, HEAD_DIM), F32)
    y, wkv = _wkv(r, lw, kh, v, kk, kka, s0, c=chunk)
    tm_mid = min(DENSE_TILE, t)
    tm_out = min(OUT_PROJ_TILE, t)
    h = _tm_post(flat(y), flat(r), flat(kh), flat(v), flat(g), flat(x), w_tm, tm=tm_mid)
    conv0 = jnp.zeros((bsz, CONV_W - 1, f), F32)
    h, conv_a = _ffn(h.reshape(bsz, t, d), conv0, w_ffn[0], g_final, step=1, tm=tm, final_norm=False)
    q, kt, v2 = _qkv_kt(h, w_sb, tm=tm_mid)
    o = _attn_prompt(q, kt, v2, qb=qb, kg=kg)
    h = _proj_res(flat(o), flat(h), w_sb['wo'], tm=tm_out)
    out, conv_b = _ffn(h.reshape(bsz, t, d), conv0, w_ffn[1], g_final, step=1, tm=tm, final_norm=True)
    n_heads = d // HEAD_DIM
    k_new = jnp.transpose(kt.reshape(bsz, n_heads, HEAD_DIM, t), (0, 3, 1, 2))
    return (out, shift.reshape(1, bsz, d), wkv[None], k_new[None], v2.reshape(bsz, t, n_heads, HEAD_DIM)[None],
            jnp.stack([conv_a, conv_b]))


def _trunk_sample(x, state_shift, state_wkv, state_conv, cache_k, cache_v, page_table, w_tm, w_sb, w_ffn, g_final,
                  *, chunk):
    bsz, t, d = x.shape
    n = bsz * t
    f = w_ffn[0]['wg'].shape[1]
    to_tm = lambda a: jnp.swapaxes(a, 0, 1).reshape(1, n, a.shape[-1])
    to_bm = lambda a: jnp.swapaxes(a.reshape(t, bsz, a.shape[-1]), 0, 1)
    xs = to_tm(x)
    r, lw, kh, v, kk, kka, g, shift = _tm_pre(xs, state_shift[0][None], w_tm, step=bsz, tm=n)
    pad_t = lambda a: jnp.pad(to_bm(a), ((0, 0), (0, chunk - t), (0, 0)))
    y, wkv = _wkv(pad_t(r), pad_t(lw), pad_t(kh), pad_t(v), pad_t(kk), pad_t(kka), state_wkv[0], c=chunk)
    y = to_tm(y[:, :t])
    flat = lambda a: a.reshape(n, d)
    h = _tm_post(flat(y), flat(r), flat(kh), flat(v), flat(g), flat(xs), w_tm, tm=n)
    conv_tm = lambda a: jnp.swapaxes(a, 0, 1).reshape(1, (CONV_W - 1) * bsz, f)
    conv_bm = lambda a: jnp.swapaxes(a.reshape(CONV_W - 1, bsz, f), 0, 1)
    h, conv_a = _ffn(h.reshape(1, n, d), conv_tm(state_conv[0]), w_ffn[0], g_final, step=bsz, tm=n,
                     final_norm=False)
    q, k, v2 = _qkv(flat(h), w_sb, tm=n)
    q, k, v2 = to_bm(q), to_bm(k), to_bm(v2)
    n_phys, page = cache_k.shape[1], cache_k.shape[2]
    keys_minor = lambda c: jnp.transpose(c[0], (0, 2, 3, 1)).reshape(n_phys, d, page)
    o = _attn_paged(q, k, v2, keys_minor(cache_k), keys_minor(cache_v), page_table)
    h = _proj_res(flat(to_tm(o)), flat(h), w_sb['wo'], tm=n)
    out, conv_b = _ffn(h.reshape(1, n, d), conv_tm(state_conv[1]), w_ffn[1], g_final, step=bsz, tm=n,
                       final_norm=True)
    hd = (bsz, t, d // HEAD_DIM, HEAD_DIM)
    return (to_bm(out[0]), shift.reshape(1, bsz, d), wkv[None], k.reshape(hd)[None], v2.reshape(hd)[None],
            jnp.stack([conv_bm(conv_a[0]), conv_bm(conv_b[0])]))


PROMPT_TILE = 256
DENSE_TILE = 512
OUT_PROJ_TILE = 1024
WKV_TILES_PER_STEP = 8
WKV_CHUNK = 64
WKV_INVERSE_BASE = 8
SAMPLE_WKV_CHUNK = 8
Q_BLOCK = 128
KEY_GROUP = 512
ATTN_QUERY_BLOCKS_PER_BODY = 4
PAGES_PER_STEP = 8


def kernel(x_prompt, x_sample, state_shift, state_wkv, cache_k, cache_v, state_conv, page_table, norm_mix, norm_ffn, norm_final, tm_mu, tm_w_r, tm_w_k, tm_w_v, tm_w_o, tm_w0, tm_w_w1, tm_w_w2, tm_a0, tm_w_a1, tm_w_a2, tm_w_g1, tm_w_g2, tm_k_k, tm_k_a, tm_r_k, tm_gn_w, tm_gn_b, sb_w_qkv, sb_b_qkv, sb_w_o, ffn_w_gate, ffn_w_up, ffn_conv_w, ffn_conv_b, ffn_w_down):
    w_tm, w_sb, w_ffn, g_final = _weights(
        norm_mix, norm_ffn, norm_final, tm_mu, tm_w_r, tm_w_k, tm_w_v, tm_w_o, tm_w0, tm_w_w1, tm_w_w2,
        tm_a0, tm_w_a1, tm_w_a2, tm_w_g1, tm_w_g2, tm_k_k, tm_k_a, tm_r_k, tm_gn_w, tm_gn_b,
        sb_w_qkv, sb_b_qkv, sb_w_o, ffn_w_gate, ffn_w_up, ffn_conv_w, ffn_conv_b, ffn_w_down)
    tile = min(PROMPT_TILE, x_prompt.shape[1])
    y_p, shift_p, wkv_p, k_p, v_p, conv_p = _trunk_prompt(
        x_prompt, w_tm, w_sb, w_ffn, g_final, tm=tile, chunk=min(WKV_CHUNK, x_prompt.shape[1]),
        qb=min(Q_BLOCK, x_prompt.shape[1]), kg=min(KEY_GROUP, x_prompt.shape[1]))
    y_s, shift_s, wkv_s, k_s, v_s, conv_s = _trunk_sample(
        x_sample, state_shift, state_wkv, state_conv, cache_k, cache_v, page_table, w_tm, w_sb, w_ffn, g_final,
        chunk=SAMPLE_WKV_CHUNK)
    return (y_p, y_s, shift_p, wkv_p, k_p, v_p, conv_p, shift_s, wkv_s, k_s, v_s, conv_s)
```
